```python
import jax, jax.numpy as jnp
from jax import lax
import numpy as np

D_MODEL = 2048
BATCH = 2
SEQ = 4096
DEPTH = 4
DEC_BATCH = 16
DEC_SEQ = 32
PAST_LEN = 4096

CHUNK = 64
Q_BLOCK = 128
N_MIXERS = 4
D_FF = 4 * D_MODEL
NORM_EPS = 1e-6

SB_HEADS = 16
SB_HD = D_MODEL // SB_HEADS

RW_HD = 64
RW_HEADS = D_MODEL // RW_HD
RW_DECAY_LORA = 96
RW_A_LORA = 96
RW_GATE_LORA = 256
RW_GN_EPS = 64e-5

MLA_HEADS = 16
MLA_Q_RANK = 1536
MLA_KV_RANK = 512
MLA_NOPE = 128
MLA_ROPE = 64
MLA_VD = 128
MLA_SCALE = (MLA_NOPE + MLA_ROPE) ** -0.5
ROPE_THETA = 10000.0

ML_HEADS = 8
ML_DQK = 128
ML_DV = 256
ML_F_BIAS = 3.0

MEM_TOKENS = 256
MEM_HEADS = 4
MEM_HD = 128

kernel_name = 'hybrid_chunk_causal_stream_step'


def rmsnorm(x, g, eps=NORM_EPS):
    xf = x.astype(jnp.float32)
    y = xf * lax.rsqrt(jnp.mean(xf * xf, axis=-1, keepdims=True) + eps)
    return (y * g.astype(jnp.float32)).astype(x.dtype)


def apply_rope(x, pos):
    half = x.shape[-1] // 2
    inv_freq = ROPE_THETA ** (-jnp.arange(half, dtype=jnp.float32) / half)
    ang = pos.astype(jnp.float32)[:, None] * inv_freq[None, :]
    ang = ang.reshape((1, pos.shape[0]) + (1,) * (x.ndim - 3) + (half,))
    cos, sin = jnp.cos(ang), jnp.sin(ang)
    xf = x.astype(jnp.float32)
    x1, x2 = xf[..., :half], xf[..., half:]
    return jnp.concatenate([x1 * cos - x2 * sin, x1 * sin + x2 * cos], axis=-1).astype(x.dtype)


def sweep_query_blocks(block_fn, q_args, kv_args, pos):
    seq = pos.shape[0]
    outs = []
    for start in range(0, seq, Q_BLOCK):
        end = start + Q_BLOCK
        outs.append(block_fn(*[a[:, start:end] for a in q_args], *[a[:, :end] for a in kv_args],
                             pos[start:end], pos[:end]))
    return jnp.concatenate(outs, axis=1)


def sb_qkv(u, w_qkv):
    b, t, _ = u.shape
    qkv = (u @ w_qkv).reshape(b, t, 3, SB_HEADS, SB_HD)
    return qkv[:, :, 0], qkv[:, :, 1], qkv[:, :, 2]


def sb_block(q, k, v, q_pos, k_pos):
    z = jnp.einsum('bqhd,bkhd->bhqk', q, k).astype(jnp.float32) * (SB_HD ** -0.5)
    earlier = k_pos[None, :] < q_pos[:, None]
    log_keep = jnp.where(earlier, jax.nn.log_sigmoid(-z), 0.0)
    suffix = lax.cumsum(log_keep, axis=3, reverse=True) - log_keep
    weights = jnp.where(earlier, jnp.exp(jax.nn.log_sigmoid(z) + suffix), 0.0)
    return jnp.einsum('bhqk,bkhd->bqhd', weights.astype(v.dtype), v)


def wkv7_scan(r, w, k, v, a, b, s0):
    def step(s, inp):
        r_t, w_t, k_t, v_t, a_t, b_t = inp
        sa = jnp.einsum('bhij,bhj->bhi', s, a_t)
        s = s * w_t[:, :, None, :] + sa[..., None] * b_t[:, :, None, :] + v_t[..., None] * k_t[:, :, None, :]
        return s, jnp.einsum('bhij,bhj->bhi', s, r_t)
    tm = lambda z: jnp.swapaxes(z, 0, 1)
    s_last, y = lax.scan(step, s0, (tm(r), tm(w), tm(k), tm(v), tm(a), tm(b)))
    return tm(y), s_last


def rwkv7_mix(u, shift0, wkv0, mu, w_r, w_k, w_v, w_o, w0, w1, w2, a0, a1, a2, g1, g2,
              k_k, k_a, r_k, ln_w, ln_b):
    b, t, d = u.shape
    prev = jnp.concatenate([shift0[:, None].astype(u.dtype), u[:, :-1]], axis=1)
    xx = prev - u
    xr, xw, xk, xv, xa, xg = [u + xx * mu[i] for i in range(6)]
    r = xr @ w_r
    k = xk @ w_k
    v = xv @ w_v
    w_log = -jax.nn.softplus(-(w0 + jnp.tanh(xw @ w1) @ w2)) - 0.5
    a = jax.nn.sigmoid(a0 + (xa @ a1) @ a2)
    g = jax.nn.sigmoid(xg @ g1) @ g2
    heads = lambda z: z.reshape(b, t, RW_HEADS, RW_HD).astype(jnp.float32)
    kk = heads(k * k_k)
    kk = kk * lax.rsqrt(jnp.maximum(jnp.sum(kk * kk, axis=-1, keepdims=True), 1e-24))
    k = k * (1 + (a - 1) * k_a)
    decay = jnp.exp(-jnp.exp(heads(w_log)))
    a_h, r_h, k_h, v_h = heads(a), heads(r), heads(k), heads(v)
    y, wkv = wkv7_scan(r_h, decay, k_h, v_h, -kk, kk * a_h, wkv0.astype(jnp.float32))
    mean = jnp.mean(y, axis=-1, keepdims=True)
    var = jnp.mean(jnp.square(y - mean), axis=-1, keepdims=True)
    y = ((y - mean) * lax.rsqrt(var + RW_GN_EPS)).reshape(b, t, d) * ln_w + ln_b
    bonus = jnp.sum(r_h * k_h * r_k, axis=-1, keepdims=True) * v_h
    y = (y + bonus.reshape(b, t, d)).astype(u.dtype)
    return (y * g) @ w_o, u[:, -1], wkv


def mla_project(u, pos, w_dq, g_q, w_uq, w_dkv, g_kv, w_uk):
    b, t, _ = u.shape
    q = (rmsnorm(u @ w_dq, g_q) @ w_uq).reshape(b, t, MLA_HEADS, MLA_NOPE + MLA_ROPE)
    q_lat = jnp.einsum('bthn,chn->bthc', q[..., :MLA_NOPE], w_uk)
    q_rope = apply_rope(q[..., MLA_NOPE:], pos)
    kv = u @ w_dkv
    c_kv = rmsnorm(kv[..., :MLA_KV_RANK], g_kv)
    k_rope = apply_rope(kv[..., MLA_KV_RANK:], pos)
    return q_lat, q_rope, c_kv, k_rope


def mla_block(q_lat, q_rope, c_kv, k_rope, q_pos, k_pos):
    s = (jnp.einsum('bqhc,bkc->bhqk', q_lat, c_kv)
         + jnp.einsum('bqhr,bkr->bhqk', q_rope, k_rope)).astype(jnp.float32) * MLA_SCALE
    visible = (k_pos // CHUNK)[None, :] <= (q_pos // CHUNK)[:, None]
    p = jax.nn.softmax(jnp.where(visible, s, -jnp.inf), axis=-1)
    return jnp.einsum('bhqk,bkc->bqhc', p.astype(c_kv.dtype), c_kv)


def mla_out(o_lat, w_uv, w_o):
    b, t = o_lat.shape[:2]
    o = jnp.einsum('bqhc,chv->bqhv', o_lat, w_uv).reshape(b, t, MLA_HEADS * MLA_VD)
    return o @ w_o


def mlstm_chunkwise(q, k, v, ig, lf, c0, n0, m0, chunk):
    b, t, h, _ = q.shape
    nc = t // chunk
    to_chunks = lambda a: jnp.swapaxes(a.astype(jnp.float32).reshape((b, nc, chunk) + a.shape[2:]), 0, 1)
    causal = jnp.tril(jnp.ones((chunk, chunk), dtype=bool))

    def step(carry, inp):
        c, n, m = carry
        q_, k_, v_, i_, f_ = inp
        cum = jnp.cumsum(f_, axis=1)
        dmat = cum[:, :, None, :] - cum[:, None, :, :] + i_[:, None, :, :]
        dmat = jnp.where(causal[None, :, :, None], dmat, -jnp.inf)
        inter = cum + m[:, None, :]
        m_t = jnp.maximum(jnp.max(dmat, axis=2), inter)
        s = jnp.einsum('bthd,bshd->btsh', q_, k_) * jnp.exp(dmat - m_t[:, :, None, :])
        inter_w = jnp.exp(inter - m_t)
        num = jnp.einsum('btsh,bshv->bthv', s, v_) + inter_w[..., None] * jnp.einsum('bthd,bhdv->bthv', q_, c)
        den = jnp.sum(s, axis=2) + inter_w * jnp.einsum('bthd,bhd->bth', q_, n)
        out = num / jnp.maximum(jnp.abs(den), jnp.exp(-m_t))[..., None]
        last = cum[:, -1]
        src = last[:, None, :] - cum + i_
        m_new = jnp.maximum(last + m, jnp.max(src, axis=1))
        keep = jnp.exp(last + m - m_new)
        wk = jnp.exp(src - m_new[:, None, :])
        c_new = keep[..., None, None] * c + jnp.einsum('bsh,bshd,bshv->bhdv', wk, k_, v_)
        n_new = keep[..., None] * n + jnp.einsum('bsh,bshd->bhd', wk, k_)
        return (c_new, n_new, m_new), out

    init = (c0.astype(jnp.float32), n0.astype(jnp.float32), m0.astype(jnp.float32))
    (c, n, m), hs = lax.scan(step, init, (to_chunks(q), to_chunks(k), to_chunks(v), to_chunks(ig), to_chunks(lf)))
    return jnp.swapaxes(hs, 0, 1).reshape(b, t, h, v.shape[-1]), c, n, m


def mlstm_mix(u, c0, n0, m0, chunk, w_in, w_if, b_if, g_h, w_out):
    b, t, _ = u.shape
    qk_w, v_w = ML_HEADS * ML_DQK, ML_HEADS * ML_DV
    proj = u @ w_in
    q = proj[..., :qk_w].reshape(b, t, ML_HEADS, ML_DQK) * (ML_DQK ** -0.5)
    k = proj[..., qk_w:2 * qk_w].reshape(b, t, ML_HEADS, ML_DQK)
    v = proj[..., 2 * qk_w:2 * qk_w + v_w].reshape(b, t, ML_HEADS, ML_DV)
    o_gate = jax.nn.sigmoid(proj[..., 2 * qk_w + v_w:])
    gates = (u @ w_if + b_if).astype(jnp.float32)
    i_pre = gates[..., :ML_HEADS]
    log_f = jax.nn.log_sigmoid(gates[..., ML_HEADS:])
    hmem, c, n, m = mlstm_chunkwise(q, k, v, i_pre, log_f, c0, n0, m0, chunk)
    hmem = rmsnorm(hmem, g_h.reshape(ML_HEADS, ML_DV)).reshape(b, t, v_w).astype(u.dtype)
    return (hmem * o_gate) @ w_out, c, n, m


def mem_project(mem, w_kv):
    b, m, _ = mem.shape
    kv = (mem @ w_kv).reshape(b, m, 2, MEM_HEADS, MEM_HD)
    return kv[:, :, 0], kv[:, :, 1]


def mem_attend(u, mk, mv, w_q, w_o):
    b, t, _ = u.shape
    q = (u @ w_q).reshape(b, t, MEM_HEADS, MEM_HD)
    s = jnp.einsum('bthd,bmhd->bhtm', q, mk).astype(jnp.float32) * (MEM_HD ** -0.5)
    p = jax.nn.softmax(s, axis=-1).astype(mv.dtype)
    o = jnp.einsum('bhtm,bmhd->bthd', p, mv).reshape(b, t, MEM_HEADS * MEM_HD)
    return o @ w_o


def sq_relu_mlp(u, w_up, w_down):
    return jnp.square(jax.nn.relu(u @ w_up)) @ w_down


def setup_inputs(seed: int = 0) -> dict:
    key = jax.random.key(seed)
    counter = [0]

    def nrm(shape, scale=1.0):
        k = jax.random.fold_in(key, counter[0])
        counter[0] += 1
        return scale * jax.random.normal(k, shape, jnp.float32)

    def gain(shape):
        return 1.0 + nrm(shape, 0.02)

    D = D_MODEL
    ml_in_w = 2 * ML_HEADS * ML_DQK + 2 * ML_HEADS * ML_DV
    mem_w = MEM_HEADS * MEM_HD
    mla_o_in = MLA_HEADS * MLA_VD
    ml_v_w = ML_HEADS * ML_DV
    return {
        'x_prompt': nrm((BATCH, SEQ, D)),
        'x_sample': nrm((DEC_BATCH, DEC_SEQ, D)),
        'mem_prompt': nrm((BATCH, MEM_TOKENS, D)),
        'cache_sb_k': nrm((DEC_BATCH, PAST_LEN, SB_HEADS, SB_HD)),
        'cache_sb_v': nrm((DEC_BATCH, PAST_LEN, SB_HEADS, SB_HD)),
        'state_rwkv_shift': nrm((DEC_BATCH, D)),
        'state_rwkv_wkv': nrm((DEC_BATCH, RW_HEADS, RW_HD, RW_HD), 0.1),
        'cache_mla_ckv': nrm((DEC_BATCH, PAST_LEN, MLA_KV_RANK)),
        'cache_mla_krope': nrm((DEC_BATCH, PAST_LEN, MLA_ROPE)),
        'state_mlstm_c': nrm((DEC_BATCH, ML_HEADS, ML_DQK, ML_DV), 0.1),
        'state_mlstm_n': nrm((DEC_BATCH, ML_HEADS, ML_DQK), 0.1),
        'state_mlstm_m': nrm((DEC_BATCH, ML_HEADS), 0.5),
        'cache_mem_k': nrm((DEPTH, DEC_BATCH, MEM_TOKENS, MEM_HEADS, MEM_HD)),
        'cache_mem_v': nrm((DEPTH, DEC_BATCH, MEM_TOKENS, MEM_HEADS, MEM_HD)),
        'norm_g': gain((DEPTH, 6, D)),
        'sb_w_qkv': nrm((D, 3 * SB_HEADS * SB_HD), D ** -0.5),
        'sb_w_o': nrm((SB_HEADS * SB_HD, D), (SB_HEADS * SB_HD) ** -0.5),
        'rw_mu': 0.5 + nrm((6, D), 0.1),
        'rw_w_r': nrm((D, D), D ** -0.5),
        'rw_w_k': nrm((D, D), D ** -0.5),
        'rw_w_v': nrm((D, D), D ** -0.5),
        'rw_w_o': nrm((D, D), D ** -0.5),
        'rw_w0': -1.0 + nrm((D,), 0.5),
        'rw_w1': nrm((D, RW_DECAY_LORA), D ** -0.5),
        'rw_w2': nrm((RW_DECAY_LORA, D), 0.1 * RW_DECAY_LORA ** -0.5),
        'rw_a0': nrm((D,), 0.1),
        'rw_a1': nrm((D, RW_A_LORA), D ** -0.5),
        'rw_a2': nrm((RW_A_LORA, D), 0.1 * RW_A_LORA ** -0.5),
        'rw_g1': nrm((D, RW_GATE_LORA), D ** -0.5),
        'rw_g2': nrm((RW_GATE_LORA, D), RW_GATE_LORA ** -0.5),
        'rw_k_k': 0.85 + nrm((D,), 0.05),
        'rw_k_a': 1.0 + nrm((D,), 0.05),
        'rw_r_k': nrm((RW_HEADS, RW_HD), 0.1),
        'rw_ln_w': gain((D,)),
        'rw_ln_b': nrm((D,), 0.02),
        'mla_w_dq': nrm((D, MLA_Q_RANK), D ** -0.5),
        'mla_g_q': gain((MLA_Q_RANK,)),
        'mla_w_uq': nrm((MLA_Q_RANK, MLA_HEADS * (MLA_NOPE + MLA_ROPE)), MLA_Q_RANK ** -0.5),
        'mla_w_dkv': nrm((D, MLA_KV_RANK + MLA_ROPE), D ** -0.5),
        'mla_g_kv': gain((MLA_KV_RANK,)),
        'mla_w_uk': nrm((MLA_KV_RANK, MLA_HEADS, MLA_NOPE), MLA_KV_RANK ** -0.5),
        'mla_w_uv': nrm((MLA_KV_RANK, MLA_HEADS, MLA_VD), MLA_KV_RANK ** -0.5),
        'mla_w_o': nrm((mla_o_in, D), mla_o_in ** -0.5),
        'ml_w_in': nrm((D, ml_in_w), D ** -0.5),
        'ml_w_if': nrm((D, 2 * ML_HEADS), D ** -0.5),
        'ml_b_if': nrm((2 * ML_HEADS,), 0.1) + jnp.repeat(jnp.array([0.0, ML_F_BIAS], jnp.float32), ML_HEADS),
        'ml_g_h': gain((ml_v_w,)),
        'ml_w_out': nrm((ml_v_w, D), ml_v_w ** -0.5),
        'w_mem_q': nrm((DEPTH, D, mem_w), D ** -0.5),
        'w_mem_kv': nrm((DEPTH, D, 2 * mem_w), D ** -0.5),
        'w_mem_o': nrm((DEPTH, mem_w, D), mem_w ** -0.5),
        'w_ffn_up': nrm((DEPTH, D, D_FF), D ** -0.5),
        'w_ffn_down': nrm((DEPTH, D_FF, D), D_FF ** -0.5),
    }


def reference(x_prompt, x_sample, mem_prompt, cache_sb_k, cache_sb_v, state_rwkv_shift, state_rwkv_wkv,
              cache_mla_ckv, cache_mla_krope, state_mlstm_c, state_mlstm_n, state_mlstm_m,
              cache_mem_k, cache_mem_v, norm_g, sb_w_qkv, sb_w_o, rw_mu, rw_w_r, rw_w_k, rw_w_v,
              rw_w_o, rw_w0, rw_w1, rw_w2, rw_a0, rw_a1, rw_a2, rw_g1, rw_g2, rw_k_k, rw_k_a,
              rw_r_k, rw_ln_w, rw_ln_b, mla_w_dq, mla_g_q, mla_w_uq, mla_w_dkv, mla_g_kv, mla_w_uk,
              mla_w_uv, mla_w_o, ml_w_in, ml_w_if, ml_b_if, ml_g_h, ml_w_out, w_mem_q, w_mem_kv,
              w_mem_o, w_ffn_up, w_ffn_down):
    f32 = jnp.float32
    bp, seq_p, d = x_prompt.shape
    bs, t_s, _ = x_sample.shape
    past = cache_sb_k.shape[1]
    pos_p = jnp.arange(seq_p)
    pos_s = past + jnp.arange(t_s)
    pos_all = jnp.arange(past + t_s)
    rw_params = (rw_mu, rw_w_r, rw_w_k, rw_w_v, rw_w_o, rw_w0, rw_w1, rw_w2, rw_a0, rw_a1, rw_a2,
                 rw_g1, rw_g2, rw_k_k, rw_k_a, rw_r_k, rw_ln_w, rw_ln_b)
    mla_params = (mla_w_dq, mla_g_q, mla_w_uq, mla_w_dkv, mla_g_kv, mla_w_uk)
    ml_params = (ml_w_in, ml_w_if, ml_b_if, ml_g_h, ml_w_out)

    hp, hs = x_prompt, x_sample
    mem_k_list, mem_v_list = [], []
    for layer in range(DEPTH):
        g = norm_g[layer]
        kind = layer % N_MIXERS
        up, us = rmsnorm(hp, g[0]), rmsnorm(hs, g[0])
        if kind == 0:
            qp, sb_k_p, sb_v_p = sb_qkv(up, sb_w_qkv)
            op = sweep_query_blocks(sb_block, (qp,), (sb_k_p, sb_v_p), pos_p)
            qs, sb_k_s, sb_v_s = sb_qkv(us, sb_w_qkv)
            os_ = sb_block(qs, jnp.concatenate([cache_sb_k, sb_k_s], axis=1),
                           jnp.concatenate([cache_sb_v, sb_v_s], axis=1), pos_s, pos_all)
            op = op.reshape(bp, seq_p, SB_HEADS * SB_HD) @ sb_w_o
            os_ = os_.reshape(bs, t_s, SB_HEADS * SB_HD) @ sb_w_o
        elif kind == 1:
            op, rw_shift_p, rw_wkv_p = rwkv7_mix(up, jnp.zeros((bp, d), f32),
                                                 jnp.zeros((bp, RW_HEADS, RW_HD, RW_HD), f32), *rw_params)
            os_, rw_shift_s, rw_wkv_s = rwkv7_mix(us, state_rwkv_shift, state_rwkv_wkv, *rw_params)
        elif kind == 2:
            qlp, qrp, mla_ckv_p, mla_krope_p = mla_project(up, pos_p, *mla_params)
            op = mla_out(sweep_query_blocks(mla_block, (qlp, qrp), (mla_ckv_p, mla_krope_p), pos_p),
                         mla_w_uv, mla_w_o)
            qls, qrs, mla_ckv_s, mla_krope_s = mla_project(us, pos_s, *mla_params)
            os_ = mla_out(mla_block(qls, qrs, jnp.concatenate([cache_mla_ckv, mla_ckv_s], axis=1),
                                    jnp.concatenate([cache_mla_krope, mla_krope_s], axis=1), pos_s, pos_all),
                          mla_w_uv, mla_w_o)
        else:
            op, ml_c_p, ml_n_p, ml_m_p = mlstm_mix(
                up, jnp.zeros((bp, ML_HEADS, ML_DQK, ML_DV), f32), jnp.zeros((bp, ML_HEADS, ML_DQK), f32),
                jnp.zeros((bp, ML_HEADS), f32), CHUNK, *ml_params)
            os_, ml_c_s, ml_n_s, ml_m_s = mlstm_mix(us, state_mlstm_c, state_mlstm_n, state_mlstm_m, t_s, *ml_params)
        hp = hp + rmsnorm(op, g[1])
        hs = hs + rmsnorm(os_, g[1])

        mkp, mvp = mem_project(mem_prompt, w_mem_kv[layer])
        mem_k_list.append(mkp)
        mem_v_list.append(mvp)
        hp = hp + rmsnorm(mem_attend(rmsnorm(hp, g[2]), mkp, mvp, w_mem_q[layer], w_mem_o[layer]), g[3])
        hs = hs + rmsnorm(mem_attend(rmsnorm(hs, g[2]), cache_mem_k[layer], cache_mem_v[layer],
                                     w_mem_q[layer], w_mem_o[layer]), g[3])

        hp = hp + rmsnorm(sq_relu_mlp(rmsnorm(hp, g[4]), w_ffn_up[layer], w_ffn_down[layer]), g[5])
        hs = hs + rmsnorm(sq_relu_mlp(rmsnorm(hs, g[4]), w_ffn_up[layer], w_ffn_down[layer]), g[5])

    mem_k_p = jnp.stack(mem_k_list, axis=0)
    mem_v_p = jnp.stack(mem_v_list, axis=0)
    return (hp, hs,
            sb_k_p, sb_v_p, sb_k_s, sb_v_s,
            rw_shift_p, rw_wkv_p, rw_shift_s, rw_wkv_s,
            mla_ckv_p, mla_krope_p, mla_ckv_s, mla_krope_s,
            ml_c_p, ml_n_p, ml_m_p, ml_c_s, ml_n_s, ml_m_s,
            mem_k_p, mem_v_p)
```

```python
import functools
import math

import jax
import jax.numpy as jnp
from jax import lax
from jax.experimental import pallas as pl
from jax.experimental.pallas import tpu as pltpu

F32, BF16 = jnp.float32, jnp.bfloat16

CHUNK = 64
NORM_EPS = 1e-6
SB_HEADS, SB_HD = 16, 128
RW_HD = 64
RW_LORA_PAD = 128
RW_GN_EPS = 64e-5
MLA_HEADS, MLA_KV_RANK, MLA_NOPE, MLA_ROPE, MLA_VD = 16, 512, 128, 64, 128
MLA_SCALE = (MLA_NOPE + MLA_ROPE) ** -0.5
MLA_KCAT = MLA_KV_RANK + 2 * MLA_ROPE
ROPE_THETA = 10000.0
ML_HEADS, ML_DQK, ML_DV = 8, 128, 256
MEM_HEADS, MEM_HD = 4, 128

V7X_VMEM_LIMIT_BYTES = 56 * 1024 * 1024
LANES = 128

NT_DIMS = (((1,), (1,)), ((), ()))


def _params(*sem):
    return pltpu.CompilerParams(dimension_semantics=sem, vmem_limit_bytes=V7X_VMEM_LIMIT_BYTES)


def _tile(n, pref, align=8):
    if n <= pref:
        return n
    for t in range(pref, 0, -1):
        if n % t == 0 and t % align == 0:
            return t
    return n


def _rms(x, g, eps=NORM_EPS):
    return x * lax.rsqrt(jnp.mean(x * x, axis=-1, keepdims=True) + eps) * g


def _dot(a, b):
    return jnp.dot(a, b, preferred_element_type=F32)


def _dot_nt(a, b):
    return lax.dot_general(a, b, NT_DIMS, preferred_element_type=F32)


def _split(x, parts):
    out = []
    for _ in range(parts):
        p = x.astype(BF16)
        out.append(p)
        x = x - p.astype(F32)
    return out


def _dot_split_lhs(x, b_bf16, parts=2):
    acc = None
    for p in _split(x, parts):
        t = _dot(p, b_bf16)
        acc = t if acc is None else acc + t
    return acc


def _dot_split_rhs(a_bf16, x, parts=2):
    acc = None
    for p in _split(x, parts):
        t = _dot(a_bf16, p)
        acc = t if acc is None else acc + t
    return acc


def _dot_hi(a, b):
    ah, al = _split(a, 2)
    bh, bl = _split(b, 2)
    return _dot(ah, bh) + (_dot(ah, bl) + _dot(al, bh))


def _softplus(z):
    return jnp.maximum(z, 0.0) + jnp.log1p(jnp.exp(-jnp.abs(z)))


def _iota(shape, dim):
    return lax.broadcasted_iota(jnp.int32, shape, dim)


def _pad_rows(x, rows):
    if x.shape[0] >= rows:
        return x
    return jnp.concatenate([x, jnp.zeros((rows - x.shape[0], x.shape[1]), x.dtype)], axis=0)


def _norm_mm_kernel(x_ref, g_ref, w_ref, *refs, normed, n_out):
    outs, u_ref = refs[:n_out], refs[n_out]

    @pl.when(pl.program_id(1) == 0)
    def _():
        x = x_ref[...].astype(F32)
        if normed:
            x = _rms(x, g_ref[...])
        u_ref[...] = x.astype(BF16)

    y = _dot(u_ref[...], w_ref[...])
    for o in outs:
        o[...] = y.astype(o.dtype)


def norm_matmul(x, g, w, out_dtypes, normed=True, tm=512, tn=512):
    m, k = x.shape
    n = w.shape[1]
    tm, tn = _tile(m, tm), _tile(n, tn, LANES)
    if g is None:
        g = jnp.ones((k,), F32)
    outs = pl.pallas_call(
        functools.partial(_norm_mm_kernel, normed=normed, n_out=len(out_dtypes)),
        grid=(m // tm, n // tn),
        in_specs=[pl.BlockSpec((tm, k), lambda i, j: (i, 0)),
                  pl.BlockSpec((1, k), lambda i, j: (0, 0)),
                  pl.BlockSpec((k, tn), lambda i, j: (0, j))],
        out_specs=[pl.BlockSpec((tm, tn), lambda i, j: (i, j)) for _ in out_dtypes],
        out_shape=[jax.ShapeDtypeStruct((m, n), d) for d in out_dtypes],
        scratch_shapes=[pltpu.VMEM((tm, k), BF16)],
        compiler_params=_params("parallel", "arbitrary"),
    )(x, g.reshape(1, k), w)
    return outs


def _mm_norm_res_kernel(a_ref, w_ref, g_ref, h_ref, o_ref):
    y = _dot(a_ref[...], w_ref[...])
    o_ref[...] = h_ref[...] + _rms(y, g_ref[...])


def matmul_norm_residual(a, w, g, h, tm=512):
    m, k = a.shape
    d = w.shape[1]
    tm = _tile(m, tm)
    return pl.pallas_call(
        _mm_norm_res_kernel,
        grid=(m // tm,),
        in_specs=[pl.BlockSpec((tm, k), lambda i: (i, 0)),
                  pl.BlockSpec((k, d), lambda i: (0, 0)),
                  pl.BlockSpec((1, d), lambda i: (0, 0)),
                  pl.BlockSpec((tm, d), lambda i: (i, 0))],
        out_specs=pl.BlockSpec((tm, d), lambda i: (i, 0)),
        out_shape=jax.ShapeDtypeStruct((m, d), F32),
        compiler_params=_params("parallel"),
    )(a, w, g.reshape(1, d), h)


def _ffn_kernel(h_ref, g1_ref, wu_ref, wd_ref, g2_ref, o_ref, u_ref, acc_ref):
    f = pl.program_id(1)

    @pl.when(f == 0)
    def _():
        u_ref[...] = _rms(h_ref[...], g1_ref[...]).astype(BF16)
        acc_ref[...] = jnp.zeros_like(acc_ref)

    a = _dot(u_ref[...], wu_ref[...])
    a = jnp.square(jnp.maximum(a, 0.0)).astype(BF16)
    acc_ref[...] += _dot(a, wd_ref[...])

    @pl.when(f == pl.num_programs(1) - 1)
    def _():
        o_ref[...] = h_ref[...] + _rms(acc_ref[...], g2_ref[...])


def ffn_block(h, g_pre, w_up, w_down, g_post, tm=512, tf=512):
    m, d = h.shape
    ff = w_up.shape[1]
    tm, tf = _tile(m, tm), _tile(ff, tf, LANES)
    return pl.pallas_call(
        _ffn_kernel,
        grid=(m // tm, ff // tf),
        in_specs=[pl.BlockSpec((tm, d), lambda i, f: (i, 0)),
                  pl.BlockSpec((1, d), lambda i, f: (0, 0)),
                  pl.BlockSpec((d, tf), lambda i, f: (0, f)),
                  pl.BlockSpec((tf, d), lambda i, f: (f, 0)),
                  pl.BlockSpec((1, d), lambda i, f: (0, 0))],
        out_specs=pl.BlockSpec((tm, d), lambda i, f: (i, 0)),
        out_shape=jax.ShapeDtypeStruct((m, d), F32),
        scratch_shapes=[pltpu.VMEM((tm, d), BF16), pltpu.VMEM((tm, d), F32)],
        compiler_params=_params("parallel", "arbitrary"),
    )(h, g_pre.reshape(1, d), w_up, w_down, g_post.reshape(1, d))


def _mem_attn_kernel(h_ref, g1_ref, wq_ref, mk_ref, mv_ref, wo_ref, g2_ref, o_ref):
    h = h_ref[0]
    u = _rms(h, g1_ref[...]).astype(BF16)
    q = (_dot(u, wq_ref[...]) * (MEM_HD ** -0.5)).astype(BF16)
    mk = mk_ref[0].astype(BF16)
    mv = mv_ref[0].astype(BF16)
    heads = []
    for hh in range(MEM_HEADS):
        sl = slice(hh * MEM_HD, (hh + 1) * MEM_HD)
        s = _dot_nt(q[:, sl], mk[:, sl])
        p = jnp.exp(s - jnp.max(s, axis=-1, keepdims=True))
        o = _dot(p.astype(BF16), mv[:, sl]) / jnp.sum(p, axis=-1, keepdims=True)
        heads.append(o.astype(BF16))
    y = _dot(jnp.concatenate(heads, axis=-1), wo_ref[...])
    o_ref[0] = h + _rms(y, g2_ref[...])


def mem_attn_block(h, g_pre, w_q, mk, mv, w_o, g_post, tm=512):
    b, t, d = h.shape
    mt, mw = mk.shape[1], mk.shape[2]
    tm = _tile(t, tm)
    return pl.pallas_call(
        _mem_attn_kernel,
        grid=(b, t // tm),
        in_specs=[pl.BlockSpec((1, tm, d), lambda bi, i: (bi, i, 0)),
                  pl.BlockSpec((1, d), lambda bi, i: (0, 0)),
                  pl.BlockSpec((d, mw), lambda bi, i: (0, 0)),
                  pl.BlockSpec((1, mt, mw), lambda bi, i: (bi, 0, 0)),
                  pl.BlockSpec((1, mt, mw), lambda bi, i: (bi, 0, 0)),
                  pl.BlockSpec((mw, d), lambda bi, i: (0, 0)),
                  pl.BlockSpec((1, d), lambda bi, i: (0, 0))],
        out_specs=pl.BlockSpec((1, tm, d), lambda bi, i: (bi, i, 0)),
        out_shape=jax.ShapeDtypeStruct((b, t, d), F32),
        compiler_params=_params("parallel", "parallel"),
    )(h, g_pre.reshape(1, d), w_q, mk, mv, w_o, g_post.reshape(1, d))


def _sb_keys_block(q, k, v, carry, mask, upper):
    z = _dot_nt(q, k) * (SB_HD ** -0.5)
    sp = _softplus(z)
    log_keep = -sp
    if mask is not None:
        log_keep = jnp.where(mask, log_keep, 0.0)
    incl = _dot_split_lhs(log_keep, upper)
    w = jnp.exp(z - sp + (incl - log_keep) + carry)
    if mask is not None:
        w = jnp.where(mask, w, 0.0)
    return _dot(w.astype(BF16), v), carry + incl[:, :1]


def _upper_ones(tk):
    return (_iota((tk, tk), 0) >= _iota((tk, tk), 1)).astype(BF16)


def _sb_prompt_kernel(q_ref, k_ref, v_ref, o_ref, acc_ref, c_ref, *, tq, tk):
    i = pl.program_id(2)
    q = q_ref[0]
    upper = _upper_ones(tk)
    acc_ref[...] = jnp.zeros_like(acc_ref)
    c_ref[...] = jnp.zeros_like(c_ref)
    nd = tq // tk
    qpos = _iota((tq, tk), 0)
    kcol = _iota((tq, tk), 1)
    for dd in reversed(range(nd)):
        start = pl.multiple_of(i * tq + dd * tk, tk)
        mask = (kcol + dd * tk) < qpos
        pv, c = _sb_keys_block(q, k_ref[0, pl.ds(start, tk), :], v_ref[0, pl.ds(start, tk), :],
                               c_ref[...], mask, upper)
        acc_ref[...] += pv
        c_ref[...] = c

    def body(jj, carry):
        start = pl.multiple_of((i * nd - 1 - jj) * tk, tk)
        pv, c = _sb_keys_block(q, k_ref[0, pl.ds(start, tk), :], v_ref[0, pl.ds(start, tk), :],
                               c_ref[...], None, upper)
        acc_ref[...] += pv
        c_ref[...] = c
        return carry

    lax.fori_loop(0, i * nd, body, 0)
    o_ref[0] = acc_ref[...].astype(o_ref.dtype)


def sb_attention_prompt(q, kv, tq=128, tk=128):
    b, t, hd = q.shape
    tq, tk = min(tq, t), min(tk, t)
    h = hd // SB_HD
    return pl.pallas_call(
        functools.partial(_sb_prompt_kernel, tq=tq, tk=tk),
        grid=(b, h, t // tq),
        in_specs=[pl.BlockSpec((1, tq, SB_HD), lambda bi, hi, i: (bi, i, hi)),
                  pl.BlockSpec((1, t, SB_HD), lambda bi, hi, i: (bi, 0, hi)),
                  pl.BlockSpec((1, t, SB_HD), lambda bi, hi, i: (bi, 0, h + hi))],
        out_specs=pl.BlockSpec((1, tq, SB_HD), lambda bi, hi, i: (bi, i, hi)),
        out_shape=jax.ShapeDtypeStruct((b, t, hd), BF16),
        scratch_shapes=[pltpu.VMEM((tq, SB_HD), F32), pltpu.VMEM((tq, 1), F32)],
        compiler_params=_params("parallel", "parallel", "arbitrary"),
    )(q, kv, kv)


def _sb_sample_kernel(q_ref, kn_ref, vn_ref, kc_ref, vc_ref, o_ref, acc_ref, c_ref, *, ts, tk, past):
    q = q_ref[0]
    mask = _iota((ts, ts), 1) < _iota((ts, ts), 0)
    pv, c = _sb_keys_block(q, kn_ref[0], vn_ref[0], jnp.zeros((ts, 1), F32), mask, _upper_ones(ts))
    acc_ref[...] = pv
    c_ref[...] = c
    upper = _upper_ones(tk)
    nblk = past // tk

    def body(jj, carry):
        start = pl.multiple_of((nblk - 1 - jj) * tk, tk)
        k = kc_ref[0, pl.ds(start, tk), :].astype(BF16)
        v = vc_ref[0, pl.ds(start, tk), :].astype(BF16)
        pv, c = _sb_keys_block(q, k, v, c_ref[...], None, upper)
        acc_ref[...] += pv
        c_ref[...] = c
        return carry

    lax.fori_loop(0, nblk, body, 0)
    o_ref[0] = acc_ref[...].astype(o_ref.dtype)


def sb_attention_sample(q, kv_new, cache_k, cache_v, tk=128):
    b, ts, hd = q.shape
    past = cache_k.shape[1]
    tk = min(tk, past)
    h = hd // SB_HD
    return pl.pallas_call(
        functools.partial(_sb_sample_kernel, ts=ts, tk=tk, past=past),
        grid=(b, h),
        in_specs=[pl.BlockSpec((1, ts, SB_HD), lambda bi, hi: (bi, 0, hi)),
                  pl.BlockSpec((1, ts, SB_HD), lambda bi, hi: (bi, 0, hi)),
                  pl.BlockSpec((1, ts, SB_HD), lambda bi, hi: (bi, 0, h + hi)),
                  pl.BlockSpec((1, past, SB_HD), lambda bi, hi: (bi, 0, hi)),
                  pl.BlockSpec((1, past, SB_HD), lambda bi, hi: (bi, 0, hi))],
        out_specs=pl.BlockSpec((1, ts, SB_HD), lambda bi, hi: (bi, 0, hi)),
        out_shape=jax.ShapeDtypeStruct((b, ts, hd), BF16),
        scratch_shapes=[pltpu.VMEM((ts, SB_HD), F32), pltpu.VMEM((ts, 1), F32)],
        compiler_params=_params("parallel", "parallel"),
    )(q, kv_new, kv_new, cache_k, cache_v)


def _rw_proj_kernel(h_ref, halo_ref, shift_ref, g_ref, mu_ref, wr_ref, wk_ref, wv_ref,
                    w1_ref, w2_ref, a1_ref, a2_ref, g1_ref, g2_ref,
                    w0_ref, a0_ref, kk_ref, ka_ref, seg_ref,
                    r_out, lw_out, k_out, v_out, a_out, b_out, g_out, last_out,
                    xr_ref, xk_ref, xv_ref, hw_ref, ha_ref, hg_ref, *, tm):
    i, j = pl.program_id(1), pl.program_id(2)

    @pl.when(j == 0)
    def _():
        u = _rms(h_ref[0], g_ref[...])
        before = _rms(halo_ref[0], g_ref[...])[7:8]
        first = jnp.where(i == 0, shift_ref[0], before)
        prev = jnp.where(_iota((tm, 1), 0) == 0, first, pltpu.roll(u, 1, 0))
        xx = prev - u
        mu = mu_ref[...]
        xr_ref[...] = (u + xx * mu[0:1]).astype(BF16)
        xw = (u + xx * mu[1:2]).astype(BF16)
        xk_ref[...] = (u + xx * mu[2:3]).astype(BF16)
        xv_ref[...] = (u + xx * mu[3:4]).astype(BF16)
        xa = (u + xx * mu[4:5]).astype(BF16)
        xg = (u + xx * mu[5:6]).astype(BF16)
        hw_ref[...] = jnp.tanh(_dot(xw, w1_ref[...])).astype(BF16)
        ha_ref[...] = _dot(xa, a1_ref[...]).astype(BF16)
        hg_ref[...] = jax.nn.sigmoid(_dot(xg, g1_ref[...])).astype(BF16)

        @pl.when(i == pl.num_programs(1) - 1)
        def _():
            last_out[0] = u[tm - 1:tm]

    r = _dot(xr_ref[...], wr_ref[...])
    k = _dot(xk_ref[...], wk_ref[...])
    v = _dot(xv_ref[...], wv_ref[...])
    w_log = -_softplus(-(w0_ref[...] + _dot(hw_ref[...], w2_ref[...]))) - 0.5
    a = jax.nn.sigmoid(a0_ref[...] + _dot(ha_ref[...], a2_ref[...]))
    gate = _dot(hg_ref[...], g2_ref[...])
    kk = k * kk_ref[...]
    ss = _dot_split_lhs(kk * kk, seg_ref[...])
    kk = kk * lax.rsqrt(jnp.maximum(ss, 1e-24))
    r_out[0] = r.astype(BF16)
    lw_out[0] = -jnp.exp(w_log)
    k_out[0] = (k * (1.0 + (a - 1.0) * ka_ref[...])).astype(BF16)
    v_out[0] = v.astype(BF16)
    a_out[0] = (-kk).astype(BF16)
    b_out[0] = (kk * a).astype(BF16)
    g_out[0] = gate.astype(BF16)


def _segment_ones(n, seg):
    idx = jnp.arange(n) // seg
    return (idx[:, None] == idx[None, :]).astype(BF16)


def rwkv_project(h, shift0, g, p, tm=512, tn=256):
    b, t, d = h.shape
    tm, tn = _tile(t, tm), _tile(d, tn, LANES)
    lp, gl = RW_LORA_PAD, p["g1"].shape[1]
    tok = lambda bi, i, j: (bi, i, j)
    col = lambda bi, i, j: (0, j)
    fix = lambda bi, i, j: (0, 0)
    tok_spec = pl.BlockSpec((1, tm, tn), tok)
    outs = pl.pallas_call(
        functools.partial(_rw_proj_kernel, tm=tm),
        grid=(b, t // tm, d // tn),
        in_specs=[pl.BlockSpec((1, tm, d), lambda bi, i, j: (bi, i, 0)),
                  pl.BlockSpec((1, 8, d), lambda bi, i, j: (bi, jnp.maximum(i * (tm // 8) - 1, 0), 0)),
                  pl.BlockSpec((1, 1, d), lambda bi, i, j: (bi, 0, 0)),
                  pl.BlockSpec((1, d), fix),
                  pl.BlockSpec((6, d), fix),
                  pl.BlockSpec((d, tn), col), pl.BlockSpec((d, tn), col), pl.BlockSpec((d, tn), col),
                  pl.BlockSpec((d, lp), fix), pl.BlockSpec((lp, tn), col),
                  pl.BlockSpec((d, lp), fix), pl.BlockSpec((lp, tn), col),
                  pl.BlockSpec((d, gl), fix), pl.BlockSpec((gl, tn), col),
                  pl.BlockSpec((1, tn), col), pl.BlockSpec((1, tn), col),
                  pl.BlockSpec((1, tn), col), pl.BlockSpec((1, tn), col),
                  pl.BlockSpec((tn, tn), fix)],
        out_specs=[tok_spec] * 7 + [pl.BlockSpec((1, 1, d), lambda bi, i, j: (bi, 0, 0))],
        out_shape=[jax.ShapeDtypeStruct((b, t, d), BF16), jax.ShapeDtypeStruct((b, t, d), F32)]
        + [jax.ShapeDtypeStruct((b, t, d), BF16)] * 5 + [jax.ShapeDtypeStruct((b, 1, d), F32)],
        scratch_shapes=[pltpu.VMEM((tm, d), BF16)] * 3
        + [pltpu.VMEM((tm, lp), BF16), pltpu.VMEM((tm, lp), BF16), pltpu.VMEM((tm, gl), BF16)],
        compiler_params=_params("parallel", "arbitrary", "arbitrary"),
    )(h, h, shift0.reshape(b, 1, d), g.reshape(1, d), p["mu"], p["w_r"], p["w_k"], p["w_v"],
      p["w1"], p["w2"], p["a1"], p["a2"], p["g1"], p["g2"],
      p["w0"].reshape(1, d), p["a0"].reshape(1, d), p["k_k"].reshape(1, d), p["k_a"].reshape(1, d),
      _segment_ones(tn, RW_HD))
    return outs


def _wkv_kernel(r_ref, lw_ref, k_ref, v_ref, a_ref, b_ref, g_ref, s0_ref, lnw_ref, lnb_ref, rk_ref,
                seg_ref, y_out, s_out, s_ref, *, L, pairs, solve_parts):
    c = pl.program_id(2)

    @pl.when(c == 0)
    def _():
        s_ref[...] = s0_ref[0]

    tt, ss = _iota((L, L), 0), _iota((L, L), 1)
    strict, incl = tt > ss, tt >= ss
    tri = incl.astype(BF16)
    eye = (tt == ss).astype(F32)
    lane = _iota((1, LANES), 1)
    head0 = lane < RW_HD
    blockdiag = (_iota((LANES, LANES), 0) // RW_HD) == (_iota((LANES, LANES), 1) // RW_HD)
    seg = seg_ref[...]
    rows2 = max(2 * L, LANES)
    steps = int(math.log2(L)) - 1

    for pp in range(pairs):
        ls = slice(pp * LANES, (pp + 1) * LANES)
        lw = lw_ref[0, :, ls]
        cum = _dot_split_rhs(tri, lw, 3)
        p_in, p_ex, p_inv = jnp.exp(cum), jnp.exp(cum - lw), jnp.exp(-cum)
        p_end = p_in[L - 1:L]
        r = r_ref[0, :, ls].astype(F32)
        k = k_ref[0, :, ls].astype(F32)
        v = v_ref[0, :, ls]
        a = a_ref[0, :, ls].astype(F32)
        b = b_ref[0, :, ls].astype(F32)
        at, bt, kt, rt = a * p_ex, b * p_inv, k * p_inv, r * p_in
        s_old = s_ref[pp]
        s_bf = s_old.astype(BF16)
        at_b, rt_b = at.astype(BF16), rt.astype(BF16)
        lhs = jnp.concatenate([jnp.where(head0, at, 0.0), jnp.where(head0, 0.0, at),
                               jnp.where(head0, rt, 0.0), jnp.where(head0, 0.0, rt)], axis=0).astype(BF16)
        gb = _dot_nt(lhs, bt.astype(BF16))
        gk = _dot_nt(lhs, kt.astype(BF16))
        at_s = _dot_nt(at_b, s_bf)
        rt_s = _dot_nt(rt_b, s_bf)
        us, ys = [], []
        for hh in range(2):
            n_ab = jnp.where(strict, gb[hh * L:(hh + 1) * L], 0.0)
            n_ak = jnp.where(strict, gk[hh * L:(hh + 1) * L], 0.0)
            w = at_s + _dot(n_ak.astype(BF16), v)
            t_acc, pw = eye + n_ab, n_ab
            for _ in range(steps):
                pw = _solve_dot(pw, pw, solve_parts)
                t_acc = t_acc + _solve_dot(t_acc, pw, solve_parts)
            us.append(_solve_dot(t_acc, w, solve_parts))
        u = jnp.where(head0, us[0], us[1])
        u_b = u.astype(BF16)
        for hh in range(2):
            m_rb = jnp.where(incl, gb[(2 + hh) * L:(3 + hh) * L], 0.0).astype(BF16)
            m_rk = jnp.where(incl, gk[(2 + hh) * L:(3 + hh) * L], 0.0).astype(BF16)
            ys.append(_dot(m_rb, u_b) + _dot(m_rk, v))
        y = rt_s + jnp.where(head0, ys[0], ys[1])
        x = _pad_rows(jnp.concatenate([bt * p_end, kt * p_end], axis=0), rows2).astype(BF16)
        z = _pad_rows(jnp.concatenate([u, v.astype(F32)], axis=0), rows2)
        upd = _dot(z.T.astype(BF16), x)
        s_new = s_old * p_end + jnp.where(blockdiag, upd, 0.0)
        s_ref[pp] = s_new
        s_out[0, pp] = s_new

        mean = _dot_split_lhs(y, seg) * (1.0 / RW_HD)
        dev = y - mean
        var = _dot_split_lhs(dev * dev, seg) * (1.0 / RW_HD)
        yn = dev * lax.rsqrt(var + RW_GN_EPS) * lnw_ref[:, ls] + lnb_ref[:, ls]
        bonus = _dot_split_lhs(r * k * rk_ref[:, ls], seg) * v.astype(F32)
        y_out[0, :, ls] = ((yn + bonus) * g_ref[0, :, ls].astype(F32)).astype(BF16)


def _solve_dot(a, b, parts):
    if parts == 1:
        return _dot(a.astype(BF16), b.astype(BF16))
    return _dot_hi(a, b)


def wkv_scan(r, lw, k, v, a, b, gate, s0, ln_w, ln_b, r_k, chunk, pairs=2, solve_parts=2):
    bsz, t, d = r.shape
    L = min(chunk, t)
    npairs = d // LANES
    pairs = min(pairs, npairs)
    w = pairs * LANES
    tok = pl.BlockSpec((1, L, w), lambda bi, pi, c: (bi, c, pi))
    vec = pl.BlockSpec((1, w), lambda bi, pi, c: (0, pi))
    st = pl.BlockSpec((1, pairs, LANES, LANES), lambda bi, pi, c: (bi, pi, 0, 0))
    y, s_fin = pl.pallas_call(
        functools.partial(_wkv_kernel, L=L, pairs=pairs, solve_parts=solve_parts),
        grid=(bsz, npairs // pairs, t // L),
        in_specs=[tok] * 7 + [st, vec, vec, vec, pl.BlockSpec((LANES, LANES), lambda bi, pi, c: (0, 0))],
        out_specs=[tok, st],
        out_shape=[jax.ShapeDtypeStruct((bsz, t, d), BF16),
                   jax.ShapeDtypeStruct((bsz, npairs, LANES, LANES), F32)],
        scratch_shapes=[pltpu.VMEM((pairs, LANES, LANES), F32)],
        compiler_params=_params("parallel", "parallel", "arbitrary"),
    )(r, lw, k, v, a, b, gate, s0, ln_w.reshape(1, d), ln_b.reshape(1, d), r_k.reshape(1, d),
      _segment_ones(LANES, RW_HD))
    return y, s_fin


def _pair_state(s):
    b, h, n, _ = s.shape
    s = s.reshape(b, h // 2, 2, n, n)
    z = jnp.zeros_like(s[:, :, 0])
    top = jnp.concatenate([s[:, :, 0], z], axis=-1)
    bot = jnp.concatenate([z, s[:, :, 1]], axis=-1)
    return jnp.concatenate([top, bot], axis=-2)


def _unpair_state(sp):
    n = RW_HD
    b, hp = sp.shape[:2]
    return jnp.stack([sp[:, :, :n, :n], sp[:, :, n:, n:]], axis=2).reshape(b, 2 * hp, n, n)


def _mla_kv_kernel(h_ref, g_ref, w_ref, gkv_ref, cos_ref, sin_ref, ckv_out, kr_out, kcat_out):
    u = _rms(h_ref[0], g_ref[...]).astype(BF16)
    kv = _dot(u, w_ref[...])
    r = MLA_KV_RANK
    c = _rms(kv[:, :r], gkv_ref[...])
    kr2 = kv[:, r:r + LANES] * cos_ref[...] + kv[:, r + LANES:r + 2 * LANES] * sin_ref[...]
    ckv_out[0] = c
    kr_out[0] = kr2[:, :MLA_ROPE]
    kcat_out[0] = jnp.concatenate([c, kr2], axis=-1).astype(BF16)


def mla_kv(h, g, w_ext, g_kv, cos2, sin2, tm=512):
    b, t, d = h.shape
    tm = _tile(t, tm)
    n = w_ext.shape[1]
    return pl.pallas_call(
        _mla_kv_kernel,
        grid=(b, t // tm),
        in_specs=[pl.BlockSpec((1, tm, d), lambda bi, i: (bi, i, 0)),
                  pl.BlockSpec((1, d), lambda bi, i: (0, 0)),
                  pl.BlockSpec((d, n), lambda bi, i: (0, 0)),
                  pl.BlockSpec((1, MLA_KV_RANK), lambda bi, i: (0, 0)),
                  pl.BlockSpec((tm, LANES), lambda bi, i: (i, 0)),
                  pl.BlockSpec((tm, LANES), lambda bi, i: (i, 0))],
        out_specs=[pl.BlockSpec((1, tm, MLA_KV_RANK), lambda bi, i: (bi, i, 0)),
                   pl.BlockSpec((1, tm, MLA_ROPE), lambda bi, i: (bi, i, 0)),
                   pl.BlockSpec((1, tm, MLA_KCAT), lambda bi, i: (bi, i, 0))],
        out_shape=[jax.ShapeDtypeStruct((b, t, MLA_KV_RANK), F32),
                   jax.ShapeDtypeStruct((b, t, MLA_ROPE), F32),
                   jax.ShapeDtypeStruct((b, t, MLA_KCAT), BF16)],
        compiler_params=_params("parallel", "parallel"),
    )(h, g.reshape(1, d), w_ext, g_kv.reshape(1, MLA_KV_RANK), cos2, sin2)


def _mla_q_kernel(nope_ref, rope_ref, rot_ref, cos_ref, sin_ref, wuk_ref, o_ref):
    hh = pl.program_id(2)
    lat = _dot(nope_ref[0], wuk_ref[0]) * MLA_SCALE
    rope = (rope_ref[0].astype(F32) * cos_ref[...] + rot_ref[0].astype(F32) * sin_ref[...]) * MLA_SCALE
    mine = (_iota((1, LANES), 1) // MLA_ROPE) == (hh % 2)
    rope = jnp.where(mine, rope, 0.0)
    o_ref[0, 0] = jnp.concatenate([lat, rope], axis=-1).astype(BF16)


def mla_q_prepare(q_all, w_uk_t, cos2, sin2, tq=512):
    b, t, _ = q_all.shape
    tq = _tile(t, tq)
    h = MLA_HEADS
    nb = h * MLA_NOPE // LANES
    rb = h * MLA_ROPE // LANES
    return pl.pallas_call(
        _mla_q_kernel,
        grid=(b, t // tq, h),
        in_specs=[pl.BlockSpec((1, tq, LANES), lambda bi, i, hh: (bi, i, hh)),
                  pl.BlockSpec((1, tq, LANES), lambda bi, i, hh: (bi, i, nb + hh // 2)),
                  pl.BlockSpec((1, tq, LANES), lambda bi, i, hh: (bi, i, nb + rb + hh // 2)),
                  pl.BlockSpec((tq, LANES), lambda bi, i, hh: (i, 0)),
                  pl.BlockSpec((tq, LANES), lambda bi, i, hh: (i, 0)),
                  pl.BlockSpec((1, MLA_NOPE, MLA_KV_RANK), lambda bi, i, hh: (hh, 0, 0))],
        out_specs=pl.BlockSpec((1, 1, tq, MLA_KCAT), lambda bi, i, hh: (bi, hh, i, 0)),
        out_shape=jax.ShapeDtypeStruct((b, h, t, MLA_KCAT), BF16),
        compiler_params=_params("parallel", "parallel", "arbitrary"),
    )(q_all, q_all, q_all, cos2, sin2, w_uk_t)


def _mla_attn_kernel(q_ref, k_ref, wuv_ref, o_ref, acc_ref, m_ref, l_ref, *, tq, tk, q0, tail, causal_blocks):
    i = pl.program_id(1)
    h = MLA_HEADS
    rows = h * tq
    q = q_ref[0].reshape(rows, MLA_KCAT)
    acc_ref[...] = jnp.zeros_like(acc_ref)
    m_ref[...] = jnp.full_like(m_ref, -jnp.inf)
    l_ref[...] = jnp.zeros_like(l_ref)
    qchunk = (q0 + i * tq + _iota((rows, 1), 0) % tq) // CHUNK

    def visit(k, start):
        n = k.shape[0]
        s = _dot_nt(q, k)
        kchunk = (start + _iota((1, n), 1)) // CHUNK
        s = jnp.where(kchunk <= qchunk, s, -jnp.inf)
        m_old = m_ref[...]
        m_new = jnp.maximum(m_old, jnp.max(s, axis=-1, keepdims=True))
        alpha = jnp.exp(m_old - m_new)
        p = jnp.exp(s - m_new)
        l_ref[...] = alpha * l_ref[...] + jnp.sum(p, axis=-1, keepdims=True)
        acc_ref[...] = alpha * acc_ref[...] + _dot(p.astype(BF16), k[:, :MLA_KV_RANK])
        m_ref[...] = m_new

    def body(j, carry):
        start = pl.multiple_of(j * tk, tk)
        visit(k_ref[0, pl.ds(start, tk), :], start)
        return carry

    nblk = (i + 1) * (tq // tk) if causal_blocks else (k_ref.shape[1] - tail) // tk
    lax.fori_loop(0, nblk, body, 0)
    if tail:
        start = k_ref.shape[1] - tail
        visit(k_ref[0, start:, :], start)
    o_lat = (acc_ref[...] / l_ref[...]).astype(BF16)
    outs = [_dot(o_lat[hh * tq:(hh + 1) * tq], wuv_ref[hh]).astype(BF16) for hh in range(h)]
    o_ref[0] = jnp.concatenate(outs, axis=-1)


def mla_attention(q_full, kcat, w_uv, q0, causal_blocks, tq=128, tk=128):
    b, h, t, _ = q_full.shape
    klen = kcat.shape[1]
    tq = _tile(t, tq, 16)
    if causal_blocks:
        tk, tail = min(tk, tq), 0
        assert tq % tk == 0 and tk % CHUNK == 0 and q0 == 0 and klen == t
    else:
        tk = min(tk, klen)
        tail = klen % tk
    return pl.pallas_call(
        functools.partial(_mla_attn_kernel, tq=tq, tk=tk, q0=q0, tail=tail, causal_blocks=causal_blocks),
        grid=(b, t // tq),
        in_specs=[pl.BlockSpec((1, h, tq, MLA_KCAT), lambda bi, i: (bi, 0, i, 0)),
                  pl.BlockSpec((1, klen, MLA_KCAT), lambda bi, i: (bi, 0, 0)),
                  pl.BlockSpec((h, MLA_KV_RANK, MLA_VD), lambda bi, i: (0, 0, 0))],
        out_specs=pl.BlockSpec((1, tq, h * MLA_VD), lambda bi, i: (bi, i, 0)),
        out_shape=jax.ShapeDtypeStruct((b, t, h * MLA_VD), BF16),
        scratch_shapes=[pltpu.VMEM((h * tq, MLA_KV_RANK), F32), pltpu.VMEM((h * tq, 1), F32),
                        pltpu.VMEM((h * tq, 1), F32)],
        compiler_params=_params("parallel", "arbitrary"),
    )(q_full, kcat, w_uv)


def _rope_tables(pos):
    half = MLA_ROPE // 2
    inv_freq = ROPE_THETA ** (-jnp.arange(half, dtype=F32) / half)
    ang = pos.astype(F32)[:, None] * inv_freq[None, :]
    cos, sin = jnp.cos(ang), jnp.sin(ang)
    return jnp.tile(cos, (1, 4)), jnp.tile(sin, (1, 4))


def _rot_cols(w):
    half = MLA_ROPE // 2
    return jnp.concatenate([-w[..., half:], w[..., :half]], axis=-1)


def _mlstm_kernel(q_ref, k_ref, v_ref, og_ref, gate_ref, bias_ref, c0_ref, n0_ref, m0_ref, gh_ref,
                  h_out, c_out, n_out, m_out, c_ref, n_ref, m_ref, *, L, heads):
    ci = pl.program_id(1)

    @pl.when(ci == 0)
    def _():
        c_ref[...] = c0_ref[0]
        n_ref[...] = n0_ref[0]
        m_ref[...] = m0_ref[0]

    tt, ss = _iota((L, L), 0), _iota((L, L), 1)
    causal = tt >= ss
    tri = causal.astype(BF16)
    scale = ML_DQK ** -0.5
    rows = max(L, LANES)

    gates = gate_ref[0] + bias_ref[...]
    log_f = -_softplus(-gates)
    cum_all = _dot_split_rhs(tri, log_f, 3)
    gates_t = _pad_rows(gates, rows).T
    cum_t = _pad_rows(cum_all, rows).T

    for hh in range(heads):
        q = q_ref[0, :, hh * ML_DQK:(hh + 1) * ML_DQK]
        k = k_ref[0, :, hh * ML_DQK:(hh + 1) * ML_DQK]
        v = v_ref[0, :, hh * ML_DV:(hh + 1) * ML_DV]
        i_c = gates[:, hh:hh + 1]
        cum_c = cum_all[:, heads + hh:heads + hh + 1]
        i_r = gates_t[hh:hh + 1, :L]
        cum_r = cum_t[heads + hh:heads + hh + 1, :L]
        m_old = m_ref[hh][:, :1]
        c_old, n_old = c_ref[hh], n_ref[hh]

        dmat = jnp.where(causal, cum_c - cum_r + i_r, -jnp.inf)
        inter = cum_c + m_old
        m_t = jnp.maximum(jnp.max(dmat, axis=1, keepdims=True), inter)
        s = _dot_nt(q, k) * scale * jnp.exp(dmat - m_t)
        inter_w = jnp.exp(inter - m_t) * scale
        num = _dot(s.astype(BF16), v) + inter_w * _dot(q, c_old.astype(BF16))
        den = jnp.sum(s, axis=1, keepdims=True) + inter_w * jnp.sum(q.astype(F32) * n_old, axis=1, keepdims=True)
        out = num / jnp.maximum(jnp.abs(den), jnp.exp(-m_t))

        last = cum_c[L - 1:L]
        src = last - cum_c + i_c
        m_new = jnp.maximum(last + m_old, jnp.max(src, axis=0, keepdims=True))
        keep = jnp.exp(last + m_old - m_new)
        kw = k.astype(F32) * jnp.exp(src - m_new)
        upd = _dot(_pad_rows(kw, rows).T.astype(BF16), _pad_rows(v, rows))
        c_new = keep * c_old + upd
        n_new = keep * n_old + jnp.sum(kw, axis=0, keepdims=True)
        c_ref[hh], n_ref[hh] = c_new, n_new
        m_ref[hh] = jnp.broadcast_to(m_new, (1, LANES))
        c_out[0, hh], n_out[0, hh] = c_new, n_new
        m_out[0, hh] = jnp.broadcast_to(m_new, (1, LANES))

        hn = _rms(out, gh_ref[:, hh * ML_DV:(hh + 1) * ML_DV])
        gate = jax.nn.sigmoid(og_ref[0, :, hh * ML_DV:(hh + 1) * ML_DV].astype(F32))
        h_out[0, :, hh * ML_DV:(hh + 1) * ML_DV] = (hn * gate).astype(BF16)


def mlstm_scan(proj, gates, bias, c0, n0, m0, g_h, chunk):
    b, t, _ = proj.shape
    h = ML_HEADS
    L = min(chunk, t)
    qw, vw = h * ML_DQK, h * ML_DV
    st = lambda shape: pl.BlockSpec((1,) + shape, lambda bi, c: (bi, 0, 0, 0))
    outs = pl.pallas_call(
        functools.partial(_mlstm_kernel, L=L, heads=h),
        grid=(b, t // L),
        in_specs=[pl.BlockSpec((1, L, qw), lambda bi, c: (bi, c, 0)),
                  pl.BlockSpec((1, L, qw), lambda bi, c: (bi, c, 1)),
                  pl.BlockSpec((1, L, vw), lambda bi, c: (bi, c, 2 * qw // vw)),
                  pl.BlockSpec((1, L, vw), lambda bi, c: (bi, c, 2 * qw // vw + 1)),
                  pl.BlockSpec((1, L, LANES), lambda bi, c: (bi, c, 0)),
                  pl.BlockSpec((1, LANES), lambda bi, c: (0, 0)),
                  st((h, ML_DQK, ML_DV)), st((h, 1, ML_DQK)), st((h, 1, LANES)),
                  pl.BlockSpec((1, vw), lambda bi, c: (0, 0))],
        out_specs=[pl.BlockSpec((1, L, vw), lambda bi, c: (bi, c, 0)),
                   st((h, ML_DQK, ML_DV)), st((h, 1, ML_DQK)), st((h, 1, LANES))],
        out_shape=[jax.ShapeDtypeStruct((b, t, vw), BF16),
                   jax.ShapeDtypeStruct((b, h, ML_DQK, ML_DV), F32),
                   jax.ShapeDtypeStruct((b, h, 1, ML_DQK), F32),
                   jax.ShapeDtypeStruct((b, h, 1, LANES), F32)],
        scratch_shapes=[pltpu.VMEM((h, ML_DQK, ML_DV), F32), pltpu.VMEM((h, 1, ML_DQK), F32),
                        pltpu.VMEM((h, 1, LANES), F32)],
        compiler_params=_params("parallel", "arbitrary"),
    )(proj, proj, proj, proj, gates, bias, c0, n0, m0, g_h.reshape(1, vw))
    return outs


def _flat(x):
    return x.reshape(-1, x.shape[-1])


def _sb_layer(hp, hs, g, w_qkv, w_o, cache_k, cache_v):
    d = hp.shape[-1]
    w_q, w_kv = w_qkv[:, :d], w_qkv[:, d:]
    res = []
    for h, caches in ((hp, None), (hs, (cache_k, cache_v))):
        b, t, _ = h.shape
        (q,) = norm_matmul(_flat(h), g[0], w_q, [BF16])
        kv32, kv16 = norm_matmul(_flat(h), g[0], w_kv, [F32, BF16])
        q, kv16 = q.reshape(b, t, d), kv16.reshape(b, t, 2 * d)
        if caches is None:
            o = sb_attention_prompt(q, kv16)
        else:
            o = sb_attention_sample(q, kv16, caches[0].reshape(b, -1, d), caches[1].reshape(b, -1, d))
        h_new = matmul_norm_residual(_flat(o), w_o, g[1], _flat(h)).reshape(b, t, d)
        kv32 = kv32.reshape(b, t, 2, SB_HEADS, SB_HD)
        res.append((h_new, kv32[:, :, 0], kv32[:, :, 1]))
    return res


def _rw_layer(h, g, shift0, wkv0, p, w_o, ln_w, ln_b, r_k, chunk):
    b, t, d = h.shape
    r, lw, k, v, a, bb, gate, last = rwkv_project(h, shift0, g[0], p)
    y, s_fin = wkv_scan(r, lw, k, v, a, bb, gate, _pair_state(wkv0), ln_w, ln_b, r_k, chunk)
    h_new = matmul_norm_residual(_flat(y), w_o, g[1], _flat(h)).reshape(b, t, d)
    return h_new, last.reshape(b, d), _unpair_state(s_fin)


def _mla_layer(h, g, pos0, kv_cache, w):
    b, t, d = h.shape
    cos2, sin2 = _rope_tables(pos0 + jnp.arange(t))
    ckv, krope, kcat = mla_kv(h, g[0], w["dkv_ext"], w["g_kv"], cos2, sin2)
    (qd,) = norm_matmul(_flat(h), g[0], w["dq"], [BF16])
    (q_all,) = norm_matmul(qd, w["g_q"], w["uq_ext"], [BF16])
    q_full = mla_q_prepare(q_all.reshape(b, t, -1), w["uk_t"], cos2, sin2)
    if kv_cache is None:
        o = mla_attention(q_full, kcat, w["uv"], 0, True)
    else:
        c_ckv, c_kr = kv_cache
        kr16 = c_kr.astype(BF16)
        k_all = jnp.concatenate([jnp.concatenate([c_ckv.astype(BF16), kr16, kr16], axis=-1), kcat], axis=1)
        o = mla_attention(q_full, k_all, w["uv"], pos0, False)
    h_new = matmul_norm_residual(_flat(o), w["o"], g[1], _flat(h)).reshape(b, t, d)
    return h_new, ckv, krope


def _ml_layer(h, g, c0, n0, m0, chunk, w):
    b, t, d = h.shape
    (proj,) = norm_matmul(_flat(h), g[0], w["in"], [BF16])
    (gates,) = norm_matmul(_flat(h), g[0], w["if_pad"], [F32])
    hm, c, n, m = mlstm_scan(proj.reshape(b, t, -1), gates.reshape(b, t, LANES), w["b_pad"],
                             c0, n0.reshape(b, ML_HEADS, 1, ML_DQK),
                             jnp.broadcast_to(m0[:, :, None, None], (b, ML_HEADS, 1, LANES)), w["g_h"], chunk)
    h_new = matmul_norm_residual(_flat(hm), w["out"], g[1], _flat(h)).reshape(b, t, d)
    return h_new, c, n.reshape(b, ML_HEADS, ML_DQK), m[:, :, 0, 0]


def kernel(x_prompt, x_sample, mem_prompt, cache_sb_k, cache_sb_v, state_rwkv_shift, state_rwkv_wkv, cache_mla_ckv, cache_mla_krope, state_mlstm_c, state_mlstm_n, state_mlstm_m, cache_mem_k, cache_mem_v, norm_g, sb_w_qkv, sb_w_o, rw_mu, rw_w_r, rw_w_k, rw_w_v, rw_w_o, rw_w0, rw_w1, rw_w2, rw_a0, rw_a1, rw_a2, rw_g1, rw_g2, rw_k_k, rw_k_a, rw_r_k, rw_ln_w, rw_ln_b, mla_w_dq, mla_g_q, mla_w_uq, mla_w_dkv, mla_g_kv, mla_w_uk, mla_w_uv, mla_w_o, ml_w_in, ml_w_if, ml_b_if, ml_g_h, ml_w_out, w_mem_q, w_mem_kv, w_mem_o, w_ffn_up, w_ffn_down):
    bp, seq_p, d = x_prompt.shape
    bs, t_s, _ = x_sample.shape
    past = cache_sb_k.shape[1]
    depth = norm_g.shape[0]
    bf = lambda w: w.astype(BF16)
    pad_cols = lambda w, n: jnp.pad(w, ((0, 0), (0, n - w.shape[1])))
    pad_rows = lambda w, n: jnp.pad(w, ((0, n - w.shape[0]), (0, 0)))

    rw_p = dict(mu=rw_mu, w_r=bf(rw_w_r), w_k=bf(rw_w_k), w_v=bf(rw_w_v),
                w1=bf(pad_cols(rw_w1, RW_LORA_PAD)), w2=bf(pad_rows(rw_w2, RW_LORA_PAD)),
                a1=bf(pad_cols(rw_a1, RW_LORA_PAD)), a2=bf(pad_rows(rw_a2, RW_LORA_PAD)),
                g1=bf(rw_g1), g2=bf(rw_g2), w0=rw_w0, a0=rw_a0, k_k=rw_k_k, k_a=rw_k_a)

    kr_w = mla_w_dkv[:, MLA_KV_RANK:]
    uq = mla_w_uq.reshape(-1, MLA_HEADS, MLA_NOPE + MLA_ROPE)
    uq_rope = uq[:, :, MLA_NOPE:]
    rank_q = uq.shape[0]
    mla_w = dict(
        dkv_ext=bf(jnp.concatenate([mla_w_dkv[:, :MLA_KV_RANK], kr_w, kr_w, _rot_cols(kr_w), _rot_cols(kr_w)], axis=1)),
        g_kv=mla_g_kv, dq=bf(mla_w_dq), g_q=mla_g_q,
        uq_ext=bf(jnp.concatenate([uq[:, :, :MLA_NOPE].reshape(rank_q, -1), uq_rope.reshape(rank_q, -1),
                                   _rot_cols(uq_rope).reshape(rank_q, -1)], axis=1)),
        uk_t=bf(jnp.transpose(mla_w_uk, (1, 2, 0))),
        uv=bf(jnp.transpose(mla_w_uv, (1, 0, 2))),
        o=bf(mla_w_o))

    ml_w = {"in": bf(ml_w_in), "if_pad": bf(pad_cols(ml_w_if, LANES)),
            "b_pad": jnp.pad(ml_b_if, (0, LANES - ml_b_if.shape[0])).reshape(1, LANES),
            "g_h": ml_g_h, "out": bf(ml_w_out)}

    hp, hs = x_prompt, x_sample
    mem_flat = _flat(mem_prompt)
    mem_k_list, mem_v_list = [], []
    outs = {}
    for layer in range(depth):
        g = norm_g[layer]
        kind = layer % 4
        if kind == 0:
            (hp, kp, vp), (hs, ks, vs) = _sb_layer(hp, hs, g, bf(sb_w_qkv), bf(sb_w_o), cache_sb_k, cache_sb_v)
            outs["sb"] = (kp, vp, ks, vs)
        elif kind == 1:
            n_h = d // RW_HD
            hp, sh_p, wkv_p = _rw_layer(hp, g, jnp.zeros((bp, d), F32), jnp.zeros((bp, n_h, RW_HD, RW_HD), F32),
                                        rw_p, bf(rw_w_o), rw_ln_w, rw_ln_b, rw_r_k, CHUNK)
            hs, sh_s, wkv_s = _rw_layer(hs, g, state_rwkv_shift, state_rwkv_wkv,
                                        rw_p, bf(rw_w_o), rw_ln_w, rw_ln_b, rw_r_k, CHUNK)
            outs["rw"] = (sh_p, wkv_p, sh_s, wkv_s)
        elif kind == 2:
            hp, ckv_p, kr_p = _mla_layer(hp, g, 0, None, mla_w)
            hs, ckv_s, kr_s = _mla_layer(hs, g, past, (cache_mla_ckv, cache_mla_krope), mla_w)
            outs["mla"] = (ckv_p, kr_p, ckv_s, kr_s)
        else:
            zc = jnp.zeros((bp, ML_HEADS, ML_DQK, ML_DV), F32)
            hp, c_p, n_p, m_p = _ml_layer(hp, g, zc, jnp.zeros((bp, ML_HEADS, ML_DQK), F32),
                                          jnp.zeros((bp, ML_HEADS), F32), CHUNK, ml_w)
            hs, c_s, n_s, m_s = _ml_layer(hs, g, state_mlstm_c, state_mlstm_n, state_mlstm_m, t_s, ml_w)
            outs["ml"] = (c_p, n_p, m_p, c_s, n_s, m_s)

        (mkv,) = norm_matmul(mem_flat, None, bf(w_mem_kv[layer]), [F32], normed=False)
        mw = mkv.shape[1] // 2
        mkv = mkv.reshape(bp, -1, 2 * mw)
        mk, mv = mkv[:, :, :mw], mkv[:, :, mw:]
        mem_k_list.append(mk.reshape(bp, -1, MEM_HEADS, MEM_HD))
        mem_v_list.append(mv.reshape(bp, -1, MEM_HEADS, MEM_HD))
        wq, wo = bf(w_mem_q[layer]), bf(w_mem_o[layer])
        hp = mem_attn_block(hp, g[2], wq, mk, mv, wo, g[3])
        hs = mem_attn_block(hs, g[2], wq, cache_mem_k[layer].reshape(bs, -1, mw),
                            cache_mem_v[layer].reshape(bs, -1, mw), wo, g[3])

        wu, wd = bf(w_ffn_up[layer]), bf(w_ffn_down[layer])
        hp = ffn_block(_flat(hp), g[4], wu, wd, g[5]).reshape(bp, seq_p, d)
        hs = ffn_block(_flat(hs), g[4], wu, wd, g[5]).reshape(bs, t_s, d)

    return (hp, hs) + outs["sb"] + outs["rw"] + outs["mla"] + outs["ml"] + (
        jnp.stack(mem_k_list, axis=0), jnp.stack(mem_v_list, axis=0))
```

```python
import functools
import math

import jax
import jax.numpy as jnp
from jax import lax
from jax.experimental import pallas as pl
from jax.experimental.pallas import tpu as pltpu

F32, BF16 = jnp.float32, jnp.bfloat16

CHUNK = 64
NORM_EPS = 1e-6
SB_HEADS, SB_HD = 16, 128
RW_HD = 64
RW_LORA_PAD = 128
RW_GN_EPS = 64e-5
MLA_HEADS, MLA_KV_RANK, MLA_NOPE, MLA_ROPE, MLA_VD = 16, 512, 128, 64, 128
MLA_SCALE = (MLA_NOPE + MLA_ROPE) ** -0.5
MLA_KCAT = MLA_KV_RANK + 2 * MLA_ROPE
ROPE_THETA = 10000.0
ML_HEADS, ML_DQK, ML_DV = 8, 128, 256
MEM_HEADS, MEM_HD = 4, 128

V7X_VMEM_LIMIT_BYTES = 56 * 1024 * 1024
LANES = 128

NT_DIMS = (((1,), (1,)), ((), ()))


def _params(*sem):
    return pltpu.CompilerParams(dimension_semantics=sem, vmem_limit_bytes=V7X_VMEM_LIMIT_BYTES)


def _tile(n, pref, align=8):
    if n <= pref:
        return n
    for t in range(pref, 0, -1):
        if n % t == 0 and t % align == 0:
            return t
    return n


def _rms(x, g, eps=NORM_EPS):
    return x * lax.rsqrt(jnp.mean(x * x, axis=-1, keepdims=True) + eps) * g


def _dot(a, b):
    return jnp.dot(a, b, preferred_element_type=F32)


def _dot_nt(a, b):
    return lax.dot_general(a, b, NT_DIMS, preferred_element_type=F32)


def _split(x, parts):
    out = []
    for _ in range(parts):
        p = x.astype(BF16)
        out.append(p)
        x = x - p.astype(F32)
    return out


def _dot_split_lhs(x, b_bf16, parts=2):
    acc = None
    for p in _split(x, parts):
        t = _dot(p, b_bf16)
        acc = t if acc is None else acc + t
    return acc


def _dot_split_rhs(a_bf16, x, parts=2):
    acc = None
    for p in _split(x, parts):
        t = _dot(a_bf16, p)
        acc = t if acc is None else acc + t
    return acc


def _dot_hi(a, b):
    ah, al = _split(a, 2)
    bh, bl = _split(b, 2)
    return _dot(ah, bh) + (_dot(ah, bl) + _dot(al, bh))


def _softplus(z):
    return jnp.maximum(z, 0.0) + jnp.log1p(jnp.exp(-jnp.abs(z)))


def _iota(shape, dim):
    return lax.broadcasted_iota(jnp.int32, shape, dim)


def _pad_rows(x, rows):
    if x.shape[0] >= rows:
        return x
    return jnp.concatenate([x, jnp.zeros((rows - x.shape[0], x.shape[1]), x.dtype)], axis=0)


def _norm_mm_kernel(x_ref, g_ref, w_ref, *refs, normed, n_out):
    outs, u_ref = refs[:n_out], refs[n_out]

    @pl.when(pl.program_id(1) == 0)
    def _():
        x = x_ref[...].astype(F32)
        if normed:
            x = _rms(x, g_ref[...])
        u_ref[...] = x.astype(BF16)

    y = _dot(u_ref[...], w_ref[...])
    for o in outs:
        o[...] = y.astype(o.dtype)


def norm_matmul(x, g, w, out_dtypes, normed=True, tm=512, tn=512):
    m, k = x.shape
    n = w.shape[1]
    tm, tn = _tile(m, tm), _tile(n, tn, LANES)
    if g is None:
        g = jnp.ones((k,), F32)
    outs = pl.pallas_call(
        functools.partial(_norm_mm_kernel, normed=normed, n_out=len(out_dtypes)), name="norm_mm",
        grid=(m // tm, n // tn),
        in_specs=[pl.BlockSpec((tm, k), lambda i, j: (i, 0)),
                  pl.BlockSpec((1, k), lambda i, j: (0, 0)),
                  pl.BlockSpec((k, tn), lambda i, j: (0, j))],
        out_specs=[pl.BlockSpec((tm, tn), lambda i, j: (i, j)) for _ in out_dtypes],
        out_shape=[jax.ShapeDtypeStruct((m, n), d) for d in out_dtypes],
        scratch_shapes=[pltpu.VMEM((tm, k), BF16)],
        compiler_params=_params("parallel", "arbitrary"),
    )(x, g.reshape(1, k), w)
    return outs


def _norm_mm_heads_kernel(x_ref, g_ref, w_ref, o4_ref, o16_ref, *, heads, hd):
    u = _rms(x_ref[0], g_ref[...]).astype(BF16)
    y = _dot(u, w_ref[...])
    o16_ref[0] = y.astype(BF16)
    for hh in range(heads):
        o4_ref[0, 0, :, hh, :] = y[:, hh * hd:(hh + 1) * hd]


def norm_matmul_heads(x, g, w, heads, tm=256):
    b, t, k = x.shape
    n = w.shape[1]
    hd = SB_HD
    wide = heads * hd
    s = n // wide
    tm = _tile(t, tm)
    return pl.pallas_call(
        functools.partial(_norm_mm_heads_kernel, heads=heads, hd=hd), name="norm_mm_heads",
        grid=(b, t // tm, s),
        in_specs=[pl.BlockSpec((1, tm, k), lambda bi, i, j: (bi, i, 0)),
                  pl.BlockSpec((1, k), lambda bi, i, j: (0, 0)),
                  pl.BlockSpec((k, wide), lambda bi, i, j: (0, j))],
        out_specs=[pl.BlockSpec((1, 1, tm, heads, hd), lambda bi, i, j: (j, bi, i, 0, 0)),
                   pl.BlockSpec((1, tm, wide), lambda bi, i, j: (bi, i, j))],
        out_shape=[jax.ShapeDtypeStruct((s, b, t, heads, hd), F32),
                   jax.ShapeDtypeStruct((b, t, n), BF16)],
        compiler_params=_params("parallel", "parallel", "arbitrary"),
    )(x, g.reshape(1, k), w)


def _mm_norm_res_kernel(a_ref, w_ref, g_ref, h_ref, o_ref):
    y = _dot(a_ref[...], w_ref[...])
    o_ref[...] = h_ref[...] + _rms(y, g_ref[...])


def matmul_norm_residual(a, w, g, h, tm=512):
    m, k = a.shape
    d = w.shape[1]
    tm = _tile(m, tm)
    return pl.pallas_call(
        _mm_norm_res_kernel, name="mm_norm_res",
        grid=(m // tm,),
        in_specs=[pl.BlockSpec((tm, k), lambda i: (i, 0)),
                  pl.BlockSpec((k, d), lambda i: (0, 0)),
                  pl.BlockSpec((1, d), lambda i: (0, 0)),
                  pl.BlockSpec((tm, d), lambda i: (i, 0))],
        out_specs=pl.BlockSpec((tm, d), lambda i: (i, 0)),
        out_shape=jax.ShapeDtypeStruct((m, d), F32),
        compiler_params=_params("parallel"),
    )(a, w, g.reshape(1, d), h)


def _ffn_kernel(h_ref, g1_ref, wu_ref, wd_ref, g2_ref, o_ref, u_ref, acc_ref):
    f = pl.program_id(1)

    @pl.when(f == 0)
    def _():
        u_ref[...] = _rms(h_ref[...], g1_ref[...]).astype(BF16)
        acc_ref[...] = jnp.zeros_like(acc_ref)

    a = _dot(u_ref[...], wu_ref[...])
    a = jnp.square(jnp.maximum(a, 0.0)).astype(BF16)
    acc_ref[...] += _dot(a, wd_ref[...])

    @pl.when(f == pl.num_programs(1) - 1)
    def _():
        o_ref[...] = h_ref[...] + _rms(acc_ref[...], g2_ref[...])


def ffn_block(h, g_pre, w_up, w_down, g_post, tm=512, tf=512):
    m, d = h.shape
    ff = w_up.shape[1]
    tm, tf = _tile(m, tm), _tile(ff, tf, LANES)
    return pl.pallas_call(
        _ffn_kernel, name="ffn",
        grid=(m // tm, ff // tf),
        in_specs=[pl.BlockSpec((tm, d), lambda i, f: (i, 0)),
                  pl.BlockSpec((1, d), lambda i, f: (0, 0)),
                  pl.BlockSpec((d, tf), lambda i, f: (0, f)),
                  pl.BlockSpec((tf, d), lambda i, f: (f, 0)),
                  pl.BlockSpec((1, d), lambda i, f: (0, 0))],
        out_specs=pl.BlockSpec((tm, d), lambda i, f: (i, 0)),
        out_shape=jax.ShapeDtypeStruct((m, d), F32),
        scratch_shapes=[pltpu.VMEM((tm, d), BF16), pltpu.VMEM((tm, d), F32)],
        compiler_params=_params("parallel", "arbitrary"),
    )(h, g_pre.reshape(1, d), w_up, w_down, g_post.reshape(1, d))


def _mem_attn_kernel(h_ref, g1_ref, wq_ref, mk_ref, mv_ref, wo_ref, g2_ref, o_ref):
    h = h_ref[0]
    u = _rms(h, g1_ref[...]).astype(BF16)
    q = (_dot(u, wq_ref[...]) * (MEM_HD ** -0.5)).astype(BF16)
    mk = mk_ref[0].astype(BF16)
    mv = mv_ref[0].astype(BF16)
    heads = []
    for hh in range(MEM_HEADS):
        sl = slice(hh * MEM_HD, (hh + 1) * MEM_HD)
        s = _dot_nt(q[:, sl], mk[:, sl])
        p = jnp.exp(s - jnp.max(s, axis=-1, keepdims=True))
        o = _dot(p.astype(BF16), mv[:, sl]) / jnp.sum(p, axis=-1, keepdims=True)
        heads.append(o.astype(BF16))
    y = _dot(jnp.concatenate(heads, axis=-1), wo_ref[...])
    o_ref[0] = h + _rms(y, g2_ref[...])


def mem_attn_block(h, g_pre, w_q, mk, mv, w_o, g_post, tm=512):
    b, t, d = h.shape
    mt, mw = mk.shape[1], mk.shape[2]
    tm = _tile(t, tm)
    return pl.pallas_call(
        _mem_attn_kernel, name="mem_attn",
        grid=(b, t // tm),
        in_specs=[pl.BlockSpec((1, tm, d), lambda bi, i: (bi, i, 0)),
                  pl.BlockSpec((1, d), lambda bi, i: (0, 0)),
                  pl.BlockSpec((d, mw), lambda bi, i: (0, 0)),
                  pl.BlockSpec((1, mt, mw), lambda bi, i: (bi, 0, 0)),
                  pl.BlockSpec((1, mt, mw), lambda bi, i: (bi, 0, 0)),
                  pl.BlockSpec((mw, d), lambda bi, i: (0, 0)),
                  pl.BlockSpec((1, d), lambda bi, i: (0, 0))],
        out_specs=pl.BlockSpec((1, tm, d), lambda bi, i: (bi, i, 0)),
        out_shape=jax.ShapeDtypeStruct((b, t, d), F32),
        compiler_params=_params("parallel", "parallel"),
    )(h, g_pre.reshape(1, d), w_q, mk, mv, w_o, g_post.reshape(1, d))


def _sb_keys_block(q, k, v, carry, mask, upper):
    z = _dot_nt(q, k) * (SB_HD ** -0.5)
    sp = _softplus(z)
    log_keep = -sp
    if mask is not None:
        log_keep = jnp.where(mask, log_keep, 0.0)
    incl = _dot_split_lhs(log_keep, upper)
    w = jnp.exp(z - sp + (incl - log_keep) + carry)
    if mask is not None:
        w = jnp.where(mask, w, 0.0)
    return _dot(w.astype(BF16), v), carry + incl[:, :1]


def _upper_ones(tk):
    return (_iota((tk, tk), 0) >= _iota((tk, tk), 1)).astype(BF16)


SB_DEAD_LOG = -104.0


def _sb_alive(visited, total, c_ref):
    return jnp.logical_and(visited < total, jnp.max(c_ref[...]) > SB_DEAD_LOG)


def _sb_prompt_kernel(q_ref, k_ref, v_ref, o_ref, acc_ref, c_ref, *, tq, tk):
    i = pl.program_id(2)
    q = q_ref[0]
    upper = _upper_ones(tk)
    acc_ref[...] = jnp.zeros_like(acc_ref)
    c_ref[...] = jnp.zeros_like(c_ref)
    nd = tq // tk
    qpos = _iota((tq, tk), 0)
    kcol = _iota((tq, tk), 1)
    for dd in reversed(range(nd)):
        start = pl.multiple_of(i * tq + dd * tk, tk)
        mask = (kcol + dd * tk) < qpos
        pv, c = _sb_keys_block(q, k_ref[0, pl.ds(start, tk), :], v_ref[0, pl.ds(start, tk), :],
                               c_ref[...], mask, upper)
        acc_ref[...] += pv
        c_ref[...] = c

    def body(jj):
        start = pl.multiple_of((i * nd - 1 - jj) * tk, tk)
        pv, c = _sb_keys_block(q, k_ref[0, pl.ds(start, tk), :], v_ref[0, pl.ds(start, tk), :],
                               c_ref[...], None, upper)
        acc_ref[...] += pv
        c_ref[...] = c
        return jj + 1

    lax.while_loop(lambda jj: _sb_alive(jj, i * nd, c_ref), body, 0)
    o_ref[0] = acc_ref[...].astype(o_ref.dtype)


def sb_attention_prompt(q, kv, tq=128, tk=128):
    b, t, hd = q.shape
    tq, tk = min(tq, t), min(tk, t)
    h = hd // SB_HD
    return pl.pallas_call(
        functools.partial(_sb_prompt_kernel, tq=tq, tk=tk), name="sb_prompt",
        grid=(b, h, t // tq),
        in_specs=[pl.BlockSpec((1, tq, SB_HD), lambda bi, hi, i: (bi, i, hi)),
                  pl.BlockSpec((1, t, SB_HD), lambda bi, hi, i: (bi, 0, hi)),
                  pl.BlockSpec((1, t, SB_HD), lambda bi, hi, i: (bi, 0, h + hi))],
        out_specs=pl.BlockSpec((1, tq, SB_HD), lambda bi, hi, i: (bi, i, hi)),
        out_shape=jax.ShapeDtypeStruct((b, t, hd), BF16),
        scratch_shapes=[pltpu.VMEM((tq, SB_HD), F32), pltpu.VMEM((tq, 1), F32)],
        compiler_params=_params("parallel", "parallel", "arbitrary"),
    )(q, kv, kv)


def _sb_sample_kernel(q_ref, kn_ref, vn_ref, kc_ref, vc_ref, o_ref, acc_ref, c_ref, *, ts, tk, past):
    q = q_ref[0]
    mask = _iota((ts, ts), 1) < _iota((ts, ts), 0)
    pv, c = _sb_keys_block(q, kn_ref[0], vn_ref[0], jnp.zeros((ts, 1), F32), mask, _upper_ones(ts))
    acc_ref[...] = pv
    c_ref[...] = c
    upper = _upper_ones(tk)
    nblk = past // tk

    def body(jj):
        start = pl.multiple_of((nblk - 1 - jj) * tk, tk)
        k = kc_ref[0, pl.ds(start, tk), :].astype(BF16)
        v = vc_ref[0, pl.ds(start, tk), :].astype(BF16)
        pv, c = _sb_keys_block(q, k, v, c_ref[...], None, upper)
        acc_ref[...] += pv
        c_ref[...] = c
        return jj + 1

    lax.while_loop(lambda jj: _sb_alive(jj, nblk, c_ref), body, 0)
    o_ref[0] = acc_ref[...].astype(o_ref.dtype)


def sb_attention_sample(q, kv_new, cache_k, cache_v, tk=128):
    b, ts, hd = q.shape
    past = cache_k.shape[1]
    tk = min(tk, past)
    h = hd // SB_HD
    return pl.pallas_call(
        functools.partial(_sb_sample_kernel, ts=ts, tk=tk, past=past), name="sb_sample",
        grid=(b, h),
        in_specs=[pl.BlockSpec((1, ts, SB_HD), lambda bi, hi: (bi, 0, hi)),
                  pl.BlockSpec((1, ts, SB_HD), lambda bi, hi: (bi, 0, hi)),
                  pl.BlockSpec((1, ts, SB_HD), lambda bi, hi: (bi, 0, h + hi)),
                  pl.BlockSpec((1, past, SB_HD), lambda bi, hi: (bi, 0, hi)),
                  pl.BlockSpec((1, past, SB_HD), lambda bi, hi: (bi, 0, hi))],
        out_specs=pl.BlockSpec((1, ts, SB_HD), lambda bi, hi: (bi, 0, hi)),
        out_shape=jax.ShapeDtypeStruct((b, ts, hd), BF16),
        scratch_shapes=[pltpu.VMEM((ts, SB_HD), F32), pltpu.VMEM((ts, 1), F32)],
        compiler_params=_params("parallel", "parallel"),
    )(q, kv_new, kv_new, cache_k, cache_v)


def _rw_proj_kernel(h_ref, halo_ref, shift_ref, g_ref, mu_ref, wr_ref, wk_ref, wv_ref,
                    w1_ref, w2_ref, a1_ref, a2_ref, g1_ref, g2_ref,
                    w0_ref, a0_ref, kk_ref, ka_ref, seg_ref,
                    r_out, lw_out, k_out, v_out, a_out, b_out, g_out, last_out,
                    xr_ref, xk_ref, xv_ref, hw_ref, ha_ref, hg_ref, *, tm):
    i, j = pl.program_id(1), pl.program_id(2)

    @pl.when(j == 0)
    def _():
        u = _rms(h_ref[0], g_ref[...])
        before = _rms(halo_ref[0], g_ref[...])[7:8]
        first = jnp.where(i == 0, shift_ref[0], before)
        prev = jnp.where(_iota((tm, 1), 0) == 0, first, pltpu.roll(u, 1, 0))
        xx = prev - u
        mu = mu_ref[...]
        xr_ref[...] = (u + xx * mu[0:1]).astype(BF16)
        xw = (u + xx * mu[1:2]).astype(BF16)
        xk_ref[...] = (u + xx * mu[2:3]).astype(BF16)
        xv_ref[...] = (u + xx * mu[3:4]).astype(BF16)
        xa = (u + xx * mu[4:5]).astype(BF16)
        xg = (u + xx * mu[5:6]).astype(BF16)
        hw_ref[...] = jnp.tanh(_dot(xw, w1_ref[...])).astype(BF16)
        ha_ref[...] = _dot(xa, a1_ref[...]).astype(BF16)
        hg_ref[...] = jax.nn.sigmoid(_dot(xg, g1_ref[...])).astype(BF16)

        @pl.when(i == pl.num_programs(1) - 1)
        def _():
            last_out[0] = u[tm - 1:tm]

    r = _dot(xr_ref[...], wr_ref[...])
    k = _dot(xk_ref[...], wk_ref[...])
    v = _dot(xv_ref[...], wv_ref[...])
    w_log = -_softplus(-(w0_ref[...] + _dot(hw_ref[...], w2_ref[...]))) - 0.5
    a = jax.nn.sigmoid(a0_ref[...] + _dot(ha_ref[...], a2_ref[...]))
    gate = _dot(hg_ref[...], g2_ref[...])
    kk = k * kk_ref[...]
    ss = _dot_split_lhs(kk * kk, seg_ref[...])
    kk = kk * lax.rsqrt(jnp.maximum(ss, 1e-24))
    r_out[0] = r.astype(BF16)
    lw_out[0] = -jnp.exp(w_log)
    k_out[0] = (k * (1.0 + (a - 1.0) * ka_ref[...])).astype(BF16)
    v_out[0] = v.astype(BF16)
    a_out[0] = (-kk).astype(BF16)
    b_out[0] = (kk * a).astype(BF16)
    g_out[0] = gate.astype(BF16)


def _segment_ones(n, seg):
    idx = jnp.arange(n) // seg
    return (idx[:, None] == idx[None, :]).astype(BF16)


def rwkv_project(h, shift0, g, p, tm=512, tn=256):
    b, t, d = h.shape
    tm, tn = _tile(t, tm), _tile(d, tn, LANES)
    lp, gl = RW_LORA_PAD, p["g1"].shape[1]
    tok = lambda bi, i, j: (bi, i, j)
    col = lambda bi, i, j: (0, j)
    fix = lambda bi, i, j: (0, 0)
    tok_spec = pl.BlockSpec((1, tm, tn), tok)
    outs = pl.pallas_call(
        functools.partial(_rw_proj_kernel, tm=tm), name="rw_proj",
        grid=(b, t // tm, d // tn),
        in_specs=[pl.BlockSpec((1, tm, d), lambda bi, i, j: (bi, i, 0)),
                  pl.BlockSpec((1, 8, d), lambda bi, i, j: (bi, jnp.maximum(i * (tm // 8) - 1, 0), 0)),
                  pl.BlockSpec((1, 1, d), lambda bi, i, j: (bi, 0, 0)),
                  pl.BlockSpec((1, d), fix),
                  pl.BlockSpec((6, d), fix),
                  pl.BlockSpec((d, tn), col), pl.BlockSpec((d, tn), col), pl.BlockSpec((d, tn), col),
                  pl.BlockSpec((d, lp), fix), pl.BlockSpec((lp, tn), col),
                  pl.BlockSpec((d, lp), fix), pl.BlockSpec((lp, tn), col),
                  pl.BlockSpec((d, gl), fix), pl.BlockSpec((gl, tn), col),
                  pl.BlockSpec((1, tn), col), pl.BlockSpec((1, tn), col),
                  pl.BlockSpec((1, tn), col), pl.BlockSpec((1, tn), col),
                  pl.BlockSpec((tn, tn), fix)],
        out_specs=[tok_spec] * 7 + [pl.BlockSpec((1, 1, d), lambda bi, i, j: (bi, 0, 0))],
        out_shape=[jax.ShapeDtypeStruct((b, t, d), BF16), jax.ShapeDtypeStruct((b, t, d), F32)]
        + [jax.ShapeDtypeStruct((b, t, d), BF16)] * 5 + [jax.ShapeDtypeStruct((b, 1, d), F32)],
        scratch_shapes=[pltpu.VMEM((tm, d), BF16)] * 3
        + [pltpu.VMEM((tm, lp), BF16), pltpu.VMEM((tm, lp), BF16), pltpu.VMEM((tm, gl), BF16)],
        compiler_params=_params("parallel", "arbitrary", "arbitrary"),
    )(h, h, shift0.reshape(b, 1, d), g.reshape(1, d), p["mu"], p["w_r"], p["w_k"], p["w_v"],
      p["w1"], p["w2"], p["a1"], p["a2"], p["g1"], p["g2"],
      p["w0"].reshape(1, d), p["a0"].reshape(1, d), p["k_k"].reshape(1, d), p["k_a"].reshape(1, d),
      _segment_ones(tn, RW_HD))
    return outs


def _wkv_kernel(r_ref, lw_ref, k_ref, v_ref, a_ref, b_ref, g_ref, s0_ref, lnw_ref, lnb_ref, rk_ref,
                seg_ref, y_out, s_out, s_ref, *, L, pairs, solve_parts):
    c = pl.program_id(2)

    @pl.when(c == 0)
    def _():
        s_ref[...] = s0_ref[0]

    L2 = 2 * L
    rr, cc = _iota((L2, L2), 0), _iota((L2, L2), 1)
    same_head = (rr // L) == (cc // L)
    strict = same_head & ((rr % L) > (cc % L))
    incl = same_head & ((rr % L) >= (cc % L))
    eye = (rr == cc).astype(F32)
    tri = (_iota((L, L), 0) >= _iota((L, L), 1)).astype(BF16)
    head0 = _iota((1, LANES), 1) < RW_HD
    blockdiag = (_iota((LANES, LANES), 0) // RW_HD) == (_iota((LANES, LANES), 1) // RW_HD)
    seg = seg_ref[...]
    rows2 = max(L2, LANES)
    steps = int(math.log2(L)) - 1
    P = range(pairs)
    lanes = [slice(pp * LANES, (pp + 1) * LANES) for pp in P]

    def stack(x):
        return jnp.concatenate([jnp.where(head0, x, 0.0), jnp.where(head0, 0.0, x)], axis=0)

    def unstack(x2):
        return x2[:L] + x2[L:]

    lw = [lw_ref[0, :, ls] for ls in lanes]
    cum = [_dot_split_rhs(tri, x, 3) for x in lw]
    p_in = [jnp.exp(x) for x in cum]
    p_ex = [jnp.exp(x - y) for x, y in zip(cum, lw)]
    p_inv = [jnp.exp(-x) for x in cum]
    p_end = [x[L - 1:L] for x in p_in]
    r = [r_ref[0, :, ls].astype(F32) for ls in lanes]
    k = [k_ref[0, :, ls].astype(F32) for ls in lanes]
    v = [v_ref[0, :, ls].astype(F32) for ls in lanes]
    at = [a_ref[0, :, ls].astype(F32) * x for ls, x in zip(lanes, p_ex)]
    bt = [b_ref[0, :, ls].astype(F32) * x for ls, x in zip(lanes, p_inv)]
    kt = [x * y for x, y in zip(k, p_inv)]
    rt = [x * y for x, y in zip(r, p_in)]
    lhs = [jnp.concatenate([stack(x), stack(y)], axis=0).astype(BF16) for x, y in zip(at, rt)]
    rhs = [jnp.concatenate([stack(x), stack(y)], axis=0).astype(BF16) for x, y in zip(bt, kt)]
    gram = [_dot_nt(x, y) for x, y in zip(lhs, rhs)]
    s_old = [s_ref[pp] for pp in P]
    on_s = [_dot_nt(x, y.astype(BF16)) for x, y in zip(lhs, s_old)]
    v2 = [stack(x).astype(BF16) for x in v]
    n_ab = [jnp.where(strict, x[:L2, :L2], 0.0) for x in gram]
    n_ak = [jnp.where(strict, x[:L2, L2:], 0.0).astype(BF16) for x in gram]
    w2 = [x[:L2] + _dot(y, z) for x, y, z in zip(on_s, n_ak, v2)]
    t_acc, pw = [eye + x for x in n_ab], n_ab
    for _ in range(steps):
        pw = [_solve_dot(x, x, solve_parts) for x in pw]
        t_acc = [x + _solve_dot(x, y, solve_parts) for x, y in zip(t_acc, pw)]
    u2 = [_solve_dot(x, y, solve_parts) for x, y in zip(t_acc, w2)]
    m_rb = [jnp.where(incl, x[L2:, :L2], 0.0).astype(BF16) for x in gram]
    m_rk = [jnp.where(incl, x[L2:, L2:], 0.0).astype(BF16) for x in gram]
    y2 = [x[L2:] + _dot(a_, b_.astype(BF16)) + _dot(c_, d_)
          for x, a_, b_, c_, d_ in zip(on_s, m_rb, u2, m_rk, v2)]
    u = [unstack(x) for x in u2]
    y = [unstack(x) for x in y2]
    x_keys = [_pad_rows(jnp.concatenate([b_ * e_, k_ * e_], axis=0), rows2).astype(BF16)
              for b_, k_, e_ in zip(bt, kt, p_end)]
    z_vals = [_pad_rows(jnp.concatenate([u_, v_], axis=0), rows2) for u_, v_ in zip(u, v)]
    upd = [_dot(z_.T.astype(BF16), x_) for z_, x_ in zip(z_vals, x_keys)]
    for pp in P:
        s_new = s_old[pp] * p_end[pp] + jnp.where(blockdiag, upd[pp], 0.0)
        s_ref[pp] = s_new
        s_out[0, pp] = s_new

    mean = [_dot_split_lhs(x, seg) * (1.0 / RW_HD) for x in y]
    dev = [x - m_ for x, m_ in zip(y, mean)]
    var = [_dot_split_lhs(x * x, seg) * (1.0 / RW_HD) for x in dev]
    bonus = [_dot_split_lhs(r_ * k_ * rk_ref[:, ls], seg) * v_ for r_, k_, v_, ls in zip(r, k, v, lanes)]
    for pp, ls in enumerate(lanes):
        yn = dev[pp] * lax.rsqrt(var[pp] + RW_GN_EPS) * lnw_ref[:, ls] + lnb_ref[:, ls]
        y_out[0, :, ls] = ((yn + bonus[pp]) * g_ref[0, :, ls].astype(F32)).astype(BF16)


def _solve_dot(a, b, parts):
    if parts == 1:
        return _dot(a.astype(BF16), b.astype(BF16))
    return _dot_hi(a, b)


def wkv_scan(r, lw, k, v, a, b, gate, s0, ln_w, ln_b, r_k, chunk, pairs=4, solve_parts=2):
    bsz, t, d = r.shape
    L = min(chunk, t)
    npairs = d // LANES
    pairs = min(pairs, npairs)
    w = pairs * LANES
    tok = pl.BlockSpec((1, L, w), lambda bi, pi, c: (bi, c, pi))
    vec = pl.BlockSpec((1, w), lambda bi, pi, c: (0, pi))
    st = pl.BlockSpec((1, pairs, LANES, LANES), lambda bi, pi, c: (bi, pi, 0, 0))
    y, s_fin = pl.pallas_call(
        functools.partial(_wkv_kernel, L=L, pairs=pairs, solve_parts=solve_parts), name="wkv",
        grid=(bsz, npairs // pairs, t // L),
        in_specs=[tok] * 7 + [st, vec, vec, vec, pl.BlockSpec((LANES, LANES), lambda bi, pi, c: (0, 0))],
        out_specs=[tok, st],
        out_shape=[jax.ShapeDtypeStruct((bsz, t, d), BF16),
                   jax.ShapeDtypeStruct((bsz, npairs, LANES, LANES), F32)],
        scratch_shapes=[pltpu.VMEM((pairs, LANES, LANES), F32)],
        compiler_params=_params("parallel", "parallel", "arbitrary"),
    )(r, lw, k, v, a, b, gate, s0, ln_w.reshape(1, d), ln_b.reshape(1, d), r_k.reshape(1, d),
      _segment_ones(LANES, RW_HD))
    return y, s_fin


def _pair_state(s):
    b, h, n, _ = s.shape
    s = s.reshape(b, h // 2, 2, n, n)
    z = jnp.zeros_like(s[:, :, 0])
    top = jnp.concatenate([s[:, :, 0], z], axis=-1)
    bot = jnp.concatenate([z, s[:, :, 1]], axis=-1)
    return jnp.concatenate([top, bot], axis=-2)


def _unpair_state(sp):
    n = RW_HD
    b, hp = sp.shape[:2]
    return jnp.stack([sp[:, :, :n, :n], sp[:, :, n:, n:]], axis=2).reshape(b, 2 * hp, n, n)


def _mla_kv_kernel(h_ref, g_ref, w_ref, gkv_ref, cos_ref, sin_ref, ckv_out, kr_out, kcat_out):
    u = _rms(h_ref[0], g_ref[...]).astype(BF16)
    kv = _dot(u, w_ref[...])
    r = MLA_KV_RANK
    c = _rms(kv[:, :r], gkv_ref[...])
    kr2 = kv[:, r:r + LANES] * cos_ref[...] + kv[:, r + LANES:r + 2 * LANES] * sin_ref[...]
    ckv_out[0] = c
    kr_out[0] = kr2[:, :MLA_ROPE]
    kcat_out[0] = jnp.concatenate([c, kr2], axis=-1).astype(BF16)


def mla_kv(h, g, w_ext, g_kv, cos2, sin2, tm=512):
    b, t, d = h.shape
    tm = _tile(t, tm)
    n = w_ext.shape[1]
    return pl.pallas_call(
        _mla_kv_kernel, name="mla_kv",
        grid=(b, t // tm),
        in_specs=[pl.BlockSpec((1, tm, d), lambda bi, i: (bi, i, 0)),
                  pl.BlockSpec((1, d), lambda bi, i: (0, 0)),
                  pl.BlockSpec((d, n), lambda bi, i: (0, 0)),
                  pl.BlockSpec((1, MLA_KV_RANK), lambda bi, i: (0, 0)),
                  pl.BlockSpec((tm, LANES), lambda bi, i: (i, 0)),
                  pl.BlockSpec((tm, LANES), lambda bi, i: (i, 0))],
        out_specs=[pl.BlockSpec((1, tm, MLA_KV_RANK), lambda bi, i: (bi, i, 0)),
                   pl.BlockSpec((1, tm, MLA_ROPE), lambda bi, i: (bi, i, 0)),
                   pl.BlockSpec((1, tm, MLA_KCAT), lambda bi, i: (bi, i, 0))],
        out_shape=[jax.ShapeDtypeStruct((b, t, MLA_KV_RANK), F32),
                   jax.ShapeDtypeStruct((b, t, MLA_ROPE), F32),
                   jax.ShapeDtypeStruct((b, t, MLA_KCAT), BF16)],
        compiler_params=_params("parallel", "parallel"),
    )(h, g.reshape(1, d), w_ext, g_kv.reshape(1, MLA_KV_RANK), cos2, sin2)


def _mla_q_kernel(nope_ref, rope_ref, rot_ref, cos_ref, sin_ref, wuk_ref, o_ref):
    hh = pl.program_id(2)
    lat = _dot(nope_ref[0], wuk_ref[0]) * MLA_SCALE
    rope = (rope_ref[0].astype(F32) * cos_ref[...] + rot_ref[0].astype(F32) * sin_ref[...]) * MLA_SCALE
    mine = (_iota((1, LANES), 1) // MLA_ROPE) == (hh % 2)
    rope = jnp.where(mine, rope, 0.0)
    o_ref[0, 0] = jnp.concatenate([lat, rope], axis=-1).astype(BF16)


def mla_q_prepare(q_all, w_uk_t, cos2, sin2, tq=512):
    b, t, _ = q_all.shape
    tq = _tile(t, tq)
    h = MLA_HEADS
    nb = h * MLA_NOPE // LANES
    rb = h * MLA_ROPE // LANES
    return pl.pallas_call(
        _mla_q_kernel, name="mla_q",
        grid=(b, t // tq, h),
        in_specs=[pl.BlockSpec((1, tq, LANES), lambda bi, i, hh: (bi, i, hh)),
                  pl.BlockSpec((1, tq, LANES), lambda bi, i, hh: (bi, i, nb + hh // 2)),
                  pl.BlockSpec((1, tq, LANES), lambda bi, i, hh: (bi, i, nb + rb + hh // 2)),
                  pl.BlockSpec((tq, LANES), lambda bi, i, hh: (i, 0)),
                  pl.BlockSpec((tq, LANES), lambda bi, i, hh: (i, 0)),
                  pl.BlockSpec((1, MLA_NOPE, MLA_KV_RANK), lambda bi, i, hh: (hh, 0, 0))],
        out_specs=pl.BlockSpec((1, 1, tq, MLA_KCAT), lambda bi, i, hh: (bi, hh, i, 0)),
        out_shape=jax.ShapeDtypeStruct((b, h, t, MLA_KCAT), BF16),
        compiler_params=_params("parallel", "parallel", "arbitrary"),
    )(q_all, q_all, q_all, cos2, sin2, w_uk_t)


def _mla_attn_kernel(q_ref, k_ref, wuv_ref, o_ref, acc_ref, m_ref, l_ref, *, tq, tk, q0, tail, causal_blocks):
    i = pl.program_id(1)
    h = MLA_HEADS
    rows = h * tq
    q = q_ref[0].reshape(rows, MLA_KCAT)
    acc_ref[...] = jnp.zeros_like(acc_ref)
    m_ref[...] = jnp.full_like(m_ref, -jnp.inf)
    l_ref[...] = jnp.zeros_like(l_ref)
    qchunk = (q0 + i * tq + _iota((rows, 1), 0) % tq) // CHUNK

    def visit(k, start, masked):
        n = k.shape[0]
        s = _dot_nt(q, k)
        if masked:
            kchunk = (start + _iota((1, n), 1)) // CHUNK
            s = jnp.where(kchunk <= qchunk, s, -jnp.inf)
        m_old = m_ref[...]
        m_new = jnp.maximum(m_old, jnp.max(s, axis=-1, keepdims=True))
        alpha = jnp.exp(m_old - m_new)
        p = jnp.exp(s - m_new)
        l_ref[...] = alpha * l_ref[...] + jnp.sum(p, axis=-1, keepdims=True)
        acc_ref[...] = alpha * acc_ref[...] + _dot(p.astype(BF16), k[:, :MLA_KV_RANK])
        m_ref[...] = m_new

    def body(masked, j, carry):
        start = pl.multiple_of(j * tk, tk)
        visit(k_ref[0, pl.ds(start, tk), :], start, masked)
        return carry

    klen = k_ref.shape[1]
    nblk = (klen - tail) // tk
    if causal_blocks:
        nblk = jnp.minimum(((i + 1) * tq + tk - 1) // tk, nblk)
    first_chunk = (q0 + i * tq) // CHUNK
    n_full = jnp.minimum(((first_chunk + 1) * CHUNK) // tk, nblk)
    lax.fori_loop(0, n_full, functools.partial(body, False), 0)
    lax.fori_loop(n_full, nblk, functools.partial(body, True), 0)
    if tail:
        start = klen - tail
        visit(k_ref[0, start:, :], start, True)
    o_lat = (acc_ref[...] / l_ref[...]).astype(BF16)
    outs = [_dot(o_lat[hh * tq:(hh + 1) * tq], wuv_ref[hh]).astype(BF16) for hh in range(h)]
    o_ref[0] = jnp.concatenate(outs, axis=-1)


def mla_attention(q_full, kcat, w_uv, q0, causal_blocks, tq=128, tk=256):
    b, h, t, _ = q_full.shape
    klen = kcat.shape[1]
    tq = _tile(t, tq, 16)
    tk = min(tk, klen)
    tail = klen % tk
    if causal_blocks:
        assert tail == 0 and tk % CHUNK == 0 and q0 == 0 and klen == t
    return pl.pallas_call(
        functools.partial(_mla_attn_kernel, tq=tq, tk=tk, q0=q0, tail=tail, causal_blocks=causal_blocks),
        name="mla_attn",
        grid=(b, t // tq),
        in_specs=[pl.BlockSpec((1, h, tq, MLA_KCAT), lambda bi, i: (bi, 0, i, 0)),
                  pl.BlockSpec((1, klen, MLA_KCAT), lambda bi, i: (bi, 0, 0)),
                  pl.BlockSpec((h, MLA_KV_RANK, MLA_VD), lambda bi, i: (0, 0, 0))],
        out_specs=pl.BlockSpec((1, tq, h * MLA_VD), lambda bi, i: (bi, i, 0)),
        out_shape=jax.ShapeDtypeStruct((b, t, h * MLA_VD), BF16),
        scratch_shapes=[pltpu.VMEM((h * tq, MLA_KV_RANK), F32), pltpu.VMEM((h * tq, 1), F32),
                        pltpu.VMEM((h * tq, 1), F32)],
        compiler_params=_params("parallel", "arbitrary"),
    )(q_full, kcat, w_uv)


def _rope_tables(pos):
    half = MLA_ROPE // 2
    inv_freq = ROPE_THETA ** (-jnp.arange(half, dtype=F32) / half)
    ang = pos.astype(F32)[:, None] * inv_freq[None, :]
    cos, sin = jnp.cos(ang), jnp.sin(ang)
    return jnp.tile(cos, (1, 4)), jnp.tile(sin, (1, 4))


def _rot_cols(w):
    half = MLA_ROPE // 2
    return jnp.concatenate([-w[..., half:], w[..., :half]], axis=-1)


def _mlstm_kernel(q_ref, k_ref, v_ref, og_ref, gate_ref, bias_ref, c0_ref, n0_ref, m0_ref, gh_ref,
                  h_out, c_out, n_out, m_out, c_ref, n_ref, m_ref, *, L, heads):
    ci = pl.program_id(1)

    @pl.when(ci == 0)
    def _():
        c_ref[...] = c0_ref[0]
        n_ref[...] = n0_ref[0]
        m_ref[...] = m0_ref[0]

    tt, ss = _iota((L, L), 0), _iota((L, L), 1)
    causal = tt >= ss
    tri = causal.astype(BF16)
    scale = ML_DQK ** -0.5
    rows = max(L, LANES)

    gates = gate_ref[0] + bias_ref[...]
    log_f = -_softplus(-gates)
    cum_all = _dot_split_rhs(tri, log_f, 3)
    gates_t = _pad_rows(gates, rows).T
    cum_t = _pad_rows(cum_all, rows).T

    for hh in range(heads):
        q = q_ref[0, :, hh * ML_DQK:(hh + 1) * ML_DQK]
        k = k_ref[0, :, hh * ML_DQK:(hh + 1) * ML_DQK]
        v = v_ref[0, :, hh * ML_DV:(hh + 1) * ML_DV]
        i_c = gates[:, hh:hh + 1]
        cum_c = cum_all[:, heads + hh:heads + hh + 1]
        i_r = gates_t[hh:hh + 1, :L]
        cum_r = cum_t[heads + hh:heads + hh + 1, :L]
        m_old = m_ref[hh][:, :1]
        c_old, n_old = c_ref[hh], n_ref[hh]

        dmat = jnp.where(causal, cum_c - cum_r + i_r, -jnp.inf)
        inter = cum_c + m_old
        m_t = jnp.maximum(jnp.max(dmat, axis=1, keepdims=True), inter)
        s = _dot_nt(q, k) * scale * jnp.exp(dmat - m_t)
        inter_w = jnp.exp(inter - m_t) * scale
        num = _dot(s.astype(BF16), v) + inter_w * _dot(q, c_old.astype(BF16))
        den = jnp.sum(s, axis=1, keepdims=True) + inter_w * jnp.sum(q.astype(F32) * n_old, axis=1, keepdims=True)
        out = num / jnp.maximum(jnp.abs(den), jnp.exp(-m_t))

        last = cum_c[L - 1:L]
        src = last - cum_c + i_c
        m_new = jnp.maximum(last + m_old, jnp.max(src, axis=0, keepdims=True))
        keep = jnp.exp(last + m_old - m_new)
        kw = k.astype(F32) * jnp.exp(src - m_new)
        upd = _dot(_pad_rows(kw, rows).T.astype(BF16), _pad_rows(v, rows))
        c_new = keep * c_old + upd
        n_new = keep * n_old + jnp.sum(kw, axis=0, keepdims=True)
        c_ref[hh], n_ref[hh] = c_new, n_new
        m_ref[hh] = jnp.broadcast_to(m_new, (1, LANES))
        c_out[0, hh], n_out[0, hh] = c_new, n_new
        m_out[0, hh] = jnp.broadcast_to(m_new, (1, LANES))

        hn = _rms(out, gh_ref[:, hh * ML_DV:(hh + 1) * ML_DV])
        gate = jax.nn.sigmoid(og_ref[0, :, hh * ML_DV:(hh + 1) * ML_DV].astype(F32))
        h_out[0, :, hh * ML_DV:(hh + 1) * ML_DV] = (hn * gate).astype(BF16)


def mlstm_scan(proj, gates, bias, c0, n0, m0, g_h, chunk):
    b, t, _ = proj.shape
    h = ML_HEADS
    L = min(chunk, t)
    qw, vw = h * ML_DQK, h * ML_DV
    st = lambda shape: pl.BlockSpec((1,) + shape, lambda bi, c: (bi, 0, 0, 0))
    outs = pl.pallas_call(
        functools.partial(_mlstm_kernel, L=L, heads=h), name="mlstm",
        grid=(b, t // L),
        in_specs=[pl.BlockSpec((1, L, qw), lambda bi, c: (bi, c, 0)),
                  pl.BlockSpec((1, L, qw), lambda bi, c: (bi, c, 1)),
                  pl.BlockSpec((1, L, vw), lambda bi, c: (bi, c, 2 * qw // vw)),
                  pl.BlockSpec((1, L, vw), lambda bi, c: (bi, c, 2 * qw // vw + 1)),
                  pl.BlockSpec((1, L, LANES), lambda bi, c: (bi, c, 0)),
                  pl.BlockSpec((1, LANES), lambda bi, c: (0, 0)),
                  st((h, ML_DQK, ML_DV)), st((h, 1, ML_DQK)), st((h, 1, LANES)),
                  pl.BlockSpec((1, vw), lambda bi, c: (0, 0))],
        out_specs=[pl.BlockSpec((1, L, vw), lambda bi, c: (bi, c, 0)),
                   st((h, ML_DQK, ML_DV)), st((h, 1, ML_DQK)), st((h, 1, LANES))],
        out_shape=[jax.ShapeDtypeStruct((b, t, vw), BF16),
                   jax.ShapeDtypeStruct((b, h, ML_DQK, ML_DV), F32),
                   jax.ShapeDtypeStruct((b, h, 1, ML_DQK), F32),
                   jax.ShapeDtypeStruct((b, h, 1, LANES), F32)],
        scratch_shapes=[pltpu.VMEM((h, ML_DQK, ML_DV), F32), pltpu.VMEM((h, 1, ML_DQK), F32),
                        pltpu.VMEM((h, 1, LANES), F32)],
        compiler_params=_params("parallel", "arbitrary"),
    )(proj, proj, proj, proj, gates, bias, c0, n0, m0, g_h.reshape(1, vw))
    return outs


def _flat(x):
    return x.reshape(-1, x.shape[-1])


def _sb_layer(hp, hs, g, w_qkv, w_o, cache_k, cache_v):
    d = hp.shape[-1]
    w_q, w_kv = w_qkv[:, :d], w_qkv[:, d:]
    res = []
    for h, caches in ((hp, None), (hs, (cache_k, cache_v))):
        b, t, _ = h.shape
        (q,) = norm_matmul(_flat(h), g[0], w_q, [BF16])
        kv32, kv16 = norm_matmul_heads(h, g[0], w_kv, SB_HEADS)
        q = q.reshape(b, t, d)
        if caches is None:
            o = sb_attention_prompt(q, kv16)
        else:
            o = sb_attention_sample(q, kv16, caches[0].reshape(b, -1, d), caches[1].reshape(b, -1, d))
        h_new = matmul_norm_residual(_flat(o), w_o, g[1], _flat(h)).reshape(b, t, d)
        res.append((h_new, kv32[0], kv32[1]))
    return res


def _rw_layer(h, g, shift0, wkv0, p, w_o, ln_w, ln_b, r_k, chunk):
    b, t, d = h.shape
    r, lw, k, v, a, bb, gate, last = rwkv_project(h, shift0, g[0], p)
    y, s_fin = wkv_scan(r, lw, k, v, a, bb, gate, _pair_state(wkv0), ln_w, ln_b, r_k, chunk)
    h_new = matmul_norm_residual(_flat(y), w_o, g[1], _flat(h)).reshape(b, t, d)
    return h_new, last.reshape(b, d), _unpair_state(s_fin)


def _mla_layer(h, g, pos0, kv_cache, w):
    b, t, d = h.shape
    cos2, sin2 = _rope_tables(pos0 + jnp.arange(t))
    ckv, krope, kcat = mla_kv(h, g[0], w["dkv_ext"], w["g_kv"], cos2, sin2)
    (qd,) = norm_matmul(_flat(h), g[0], w["dq"], [BF16])
    (q_all,) = norm_matmul(qd, w["g_q"], w["uq_ext"], [BF16])
    q_full = mla_q_prepare(q_all.reshape(b, t, -1), w["uk_t"], cos2, sin2)
    if kv_cache is None:
        o = mla_attention(q_full, kcat, w["uv"], 0, True)
    else:
        c_ckv, c_kr = kv_cache
        kr16 = c_kr.astype(BF16)
        k_all = jnp.concatenate([jnp.concatenate([c_ckv.astype(BF16), kr16, kr16], axis=-1), kcat], axis=1)
        o = mla_attention(q_full, k_all, w["uv"], pos0, False)
    h_new = matmul_norm_residual(_flat(o), w["o"], g[1], _flat(h)).reshape(b, t, d)
    return h_new, ckv, krope


def _ml_layer(h, g, c0, n0, m0, chunk, w):
    b, t, d = h.shape
    (proj,) = norm_matmul(_flat(h), g[0], w["in"], [BF16])
    (gates,) = norm_matmul(_flat(h), g[0], w["if_pad"], [F32])
    hm, c, n, m = mlstm_scan(proj.reshape(b, t, -1), gates.reshape(b, t, LANES), w["b_pad"],
                             c0, n0.reshape(b, ML_HEADS, 1, ML_DQK),
                             jnp.broadcast_to(m0[:, :, None, None], (b, ML_HEADS, 1, LANES)), w["g_h"], chunk)
    h_new = matmul_norm_residual(_flat(hm), w["out"], g[1], _flat(h)).reshape(b, t, d)
    return h_new, c, n.reshape(b, ML_HEADS, ML_DQK), m[:, :, 0, 0]


def kernel(x_prompt, x_sample, mem_prompt, cache_sb_k, cache_sb_v, state_rwkv_shift, state_rwkv_wkv, cache_mla_ckv, cache_mla_krope, state_mlstm_c, state_mlstm_n, state_mlstm_m, cache_mem_k, cache_mem_v, norm_g, sb_w_qkv, sb_w_o, rw_mu, rw_w_r, rw_w_k, rw_w_v, rw_w_o, rw_w0, rw_w1, rw_w2, rw_a0, rw_a1, rw_a2, rw_g1, rw_g2, rw_k_k, rw_k_a, rw_r_k, rw_ln_w, rw_ln_b, mla_w_dq, mla_g_q, mla_w_uq, mla_w_dkv, mla_g_kv, mla_w_uk, mla_w_uv, mla_w_o, ml_w_in, ml_w_if, ml_b_if, ml_g_h, ml_w_out, w_mem_q, w_mem_kv, w_mem_o, w_ffn_up, w_ffn_down):
    bp, seq_p, d = x_prompt.shape
    bs, t_s, _ = x_sample.shape
    past = cache_sb_k.shape[1]
    depth = norm_g.shape[0]
    bf = lambda w: w.astype(BF16)
    pad_cols = lambda w, n: jnp.pad(w, ((0, 0), (0, n - w.shape[1])))
    pad_rows = lambda w, n: jnp.pad(w, ((0, n - w.shape[0]), (0, 0)))

    rw_p = dict(mu=rw_mu, w_r=bf(rw_w_r), w_k=bf(rw_w_k), w_v=bf(rw_w_v),
                w1=bf(pad_cols(rw_w1, RW_LORA_PAD)), w2=bf(pad_rows(rw_w2, RW_LORA_PAD)),
                a1=bf(pad_cols(rw_a1, RW_LORA_PAD)), a2=bf(pad_rows(rw_a2, RW_LORA_PAD)),
                g1=bf(rw_g1), g2=bf(rw_g2), w0=rw_w0, a0=rw_a0, k_k=rw_k_k, k_a=rw_k_a)

    kr_w = mla_w_dkv[:, MLA_KV_RANK:]
    uq = mla_w_uq.reshape(-1, MLA_HEADS, MLA_NOPE + MLA_ROPE)
    uq_rope = uq[:, :, MLA_NOPE:]
    rank_q = uq.shape[0]
    mla_w = dict(
        dkv_ext=bf(jnp.concatenate([mla_w_dkv[:, :MLA_KV_RANK], kr_w, kr_w, _rot_cols(kr_w), _rot_cols(kr_w)], axis=1)),
        g_kv=mla_g_kv, dq=bf(mla_w_dq), g_q=mla_g_q,
        uq_ext=bf(jnp.concatenate([uq[:, :, :MLA_NOPE].reshape(rank_q, -1), uq_rope.reshape(rank_q, -1),
                                   _rot_cols(uq_rope).reshape(rank_q, -1)], axis=1)),
        uk_t=bf(jnp.transpose(mla_w_uk, (1, 2, 0))),
        uv=bf(jnp.transpose(mla_w_uv, (1, 0, 2))),
        o=bf(mla_w_o))

    ml_w = {"in": bf(ml_w_in), "if_pad": bf(pad_cols(ml_w_if, LANES)),
            "b_pad": jnp.pad(ml_b_if, (0, LANES - ml_b_if.shape[0])).reshape(1, LANES),
            "g_h": ml_g_h, "out": bf(ml_w_out)}

    hp, hs = x_prompt, x_sample
    mem_flat = _flat(mem_prompt)
    mem_k_list, mem_v_list = [], []
    outs = {}
    for layer in range(depth):
        g = norm_g[layer]
        kind = layer % 4
        if kind == 0:
            (hp, kp, vp), (hs, ks, vs) = _sb_layer(hp, hs, g, bf(sb_w_qkv), bf(sb_w_o), cache_sb_k, cache_sb_v)
            outs["sb"] = (kp, vp, ks, vs)
        elif kind == 1:
            n_h = d // RW_HD
            hp, sh_p, wkv_p = _rw_layer(hp, g, jnp.zeros((bp, d), F32), jnp.zeros((bp, n_h, RW_HD, RW_HD), F32),
                                        rw_p, bf(rw_w_o), rw_ln_w, rw_ln_b, rw_r_k, CHUNK)
            hs, sh_s, wkv_s = _rw_layer(hs, g, state_rwkv_shift, state_rwkv_wkv,
                                        rw_p, bf(rw_w_o), rw_ln_w, rw_ln_b, rw_r_k, CHUNK)
            outs["rw"] = (sh_p, wkv_p, sh_s, wkv_s)
        elif kind == 2:
            hp, ckv_p, kr_p = _mla_layer(hp, g, 0, None, mla_w)
            hs, ckv_s, kr_s = _mla_layer(hs, g, past, (cache_mla_ckv, cache_mla_krope), mla_w)
            outs["mla"] = (ckv_p, kr_p, ckv_s, kr_s)
        else:
            zc = jnp.zeros((bp, ML_HEADS, ML_DQK, ML_DV), F32)
            hp, c_p, n_p, m_p = _ml_layer(hp, g, zc, jnp.zeros((bp, ML_HEADS, ML_DQK), F32),
                                          jnp.zeros((bp, ML_HEADS), F32), CHUNK, ml_w)
            hs, c_s, n_s, m_s = _ml_layer(hs, g, state_mlstm_c, state_mlstm_n, state_mlstm_m, t_s, ml_w)
            outs["ml"] = (c_p, n_p, m_p, c_s, n_s, m_s)

        (mkv,) = norm_matmul(mem_flat, None, bf(w_mem_kv[layer]), [F32], normed=False)
        mw = mkv.shape[1] // 2
        mkv = mkv.reshape(bp, -1, 2 * mw)
        mk, mv = mkv[:, :, :mw], mkv[:, :, mw:]
        mem_k_list.append(mk.reshape(bp, -1, MEM_HEADS, MEM_HD))
        mem_v_list.append(mv.reshape(bp, -1, MEM_HEADS, MEM_HD))
        wq, wo = bf(w_mem_q[layer]), bf(w_mem_o[layer])
        hp = mem_attn_block(hp, g[2], wq, mk, mv, wo, g[3])
        hs = mem_attn_block(hs, g[2], wq, cache_mem_k[layer].reshape(bs, -1, mw),
                            cache_mem_v[layer].reshape(bs, -1, mw), wo, g[3])

        wu, wd = bf(w_ffn_up[layer]), bf(w_ffn_down[layer])
        hp = ffn_block(_flat(hp), g[4], wu, wd, g[5]).reshape(bp, seq_p, d)
        hs = ffn_block(_flat(hs), g[4], wu, wd, g[5]).reshape(bs, t_s, d)

    return (hp, hs) + outs["sb"] + outs["rw"] + outs["mla"] + outs["ml"] + (
        jnp.stack(mem_k_list, axis=0), jnp.stack(mem_v_list, axis=0))
```

```python
import functools
import math

import jax
import jax.numpy as jnp
from jax import lax
from jax.experimental import pallas as pl
from jax.experimental.pallas import tpu as pltpu

F32, BF16 = jnp.float32, jnp.bfloat16

CHUNK = 64
NORM_EPS = 1e-6
SB_HEADS, SB_HD = 16, 128
RW_HD = 64
RW_LORA_PAD = 128
RW_GN_EPS = 64e-5
MLA_HEADS, MLA_KV_RANK, MLA_NOPE, MLA_ROPE, MLA_VD = 16, 512, 128, 64, 128
MLA_SCALE = (MLA_NOPE + MLA_ROPE) ** -0.5
MLA_KCAT = MLA_KV_RANK + 2 * MLA_ROPE
ROPE_THETA = 10000.0
ML_HEADS, ML_DQK, ML_DV = 8, 128, 256
MEM_HEADS, MEM_HD = 4, 128

V7X_VMEM_LIMIT_BYTES = 56 * 1024 * 1024
LANES = 128

NT_DIMS = (((1,), (1,)), ((), ()))


def _params(*sem):
    return pltpu.CompilerParams(dimension_semantics=sem, vmem_limit_bytes=V7X_VMEM_LIMIT_BYTES)


def _tile(n, pref, align=8):
    if n <= pref:
        return n
    for t in range(pref, 0, -1):
        if n % t == 0 and t % align == 0:
            return t
    return n


def _rms(x, g, eps=NORM_EPS):
    return x * lax.rsqrt(jnp.mean(x * x, axis=-1, keepdims=True) + eps) * g


def _dot(a, b):
    return jnp.dot(a, b, preferred_element_type=F32)


def _dot_nt(a, b):
    return lax.dot_general(a, b, NT_DIMS, preferred_element_type=F32)


def _split(x, parts):
    out = []
    for _ in range(parts):
        p = x.astype(BF16)
        out.append(p)
        x = x - p.astype(F32)
    return out


def _dot_split_lhs(x, b_bf16, parts=2):
    acc = None
    for p in _split(x, parts):
        t = _dot(p, b_bf16)
        acc = t if acc is None else acc + t
    return acc


def _dot_split_rhs(a_bf16, x, parts=2):
    acc = None
    for p in _split(x, parts):
        t = _dot(a_bf16, p)
        acc = t if acc is None else acc + t
    return acc


def _dot_hi(a, b):
    ah, al = _split(a, 2)
    bh, bl = _split(b, 2)
    return _dot(ah, bh) + (_dot(ah, bl) + _dot(al, bh))


def _softplus(z):
    return jnp.maximum(z, 0.0) + jnp.log1p(jnp.exp(-jnp.abs(z)))


def _iota(shape, dim):
    return lax.broadcasted_iota(jnp.int32, shape, dim)


def _pad_rows(x, rows):
    if x.shape[0] >= rows:
        return x
    return jnp.concatenate([x, jnp.zeros((rows - x.shape[0], x.shape[1]), x.dtype)], axis=0)


def _norm_mm_kernel(x_ref, g_ref, w_ref, *refs, normed, n_out):
    outs, u_ref = refs[:n_out], refs[n_out]

    @pl.when(pl.program_id(1) == 0)
    def _():
        x = x_ref[...].astype(F32)
        if normed:
            x = _rms(x, g_ref[...])
        u_ref[...] = x.astype(BF16)

    y = _dot(u_ref[...], w_ref[...])
    for o in outs:
        o[...] = y.astype(o.dtype)


def norm_matmul(x, g, w, out_dtypes, normed=True, tm=512, tn=512):
    m, k = x.shape
    n = w.shape[1]
    tm, tn = _tile(m, tm), _tile(n, tn, LANES)
    if g is None:
        g = jnp.ones((k,), F32)
    outs = pl.pallas_call(
        functools.partial(_norm_mm_kernel, normed=normed, n_out=len(out_dtypes)), name="norm_mm",
        grid=(m // tm, n // tn),
        in_specs=[pl.BlockSpec((tm, k), lambda i, j: (i, 0)),
                  pl.BlockSpec((1, k), lambda i, j: (0, 0)),
                  pl.BlockSpec((k, tn), lambda i, j: (0, j))],
        out_specs=[pl.BlockSpec((tm, tn), lambda i, j: (i, j)) for _ in out_dtypes],
        out_shape=[jax.ShapeDtypeStruct((m, n), d) for d in out_dtypes],
        scratch_shapes=[pltpu.VMEM((tm, k), BF16)],
        compiler_params=_params("parallel", "arbitrary"),
    )(x, g.reshape(1, k), w)
    return outs


def _norm_mm_heads_kernel(x_ref, g_ref, w_ref, o4_ref, o16_ref, *, heads, hd, normed):
    x = x_ref[0]
    u = (_rms(x, g_ref[...]) if normed else x).astype(BF16)
    y = _dot(u, w_ref[...])
    o16_ref[0] = y.astype(BF16)
    for hh in range(heads):
        o4_ref[0, 0, :, hh, :] = y[:, hh * hd:(hh + 1) * hd]


def norm_matmul_heads(x, g, w, heads, tm=256):
    b, t, k = x.shape
    n = w.shape[1]
    hd = SB_HD
    wide = heads * hd
    s = n // wide
    tm = _tile(t, tm)
    normed = g is not None
    if not normed:
        g = jnp.ones((k,), F32)
    return pl.pallas_call(
        functools.partial(_norm_mm_heads_kernel, heads=heads, hd=hd, normed=normed), name="norm_mm_heads",
        grid=(b, t // tm, s),
        in_specs=[pl.BlockSpec((1, tm, k), lambda bi, i, j: (bi, i, 0)),
                  pl.BlockSpec((1, k), lambda bi, i, j: (0, 0)),
                  pl.BlockSpec((k, wide), lambda bi, i, j: (0, j))],
        out_specs=[pl.BlockSpec((1, 1, tm, heads, hd), lambda bi, i, j: (j, bi, i, 0, 0)),
                   pl.BlockSpec((1, tm, wide), lambda bi, i, j: (bi, i, j))],
        out_shape=[jax.ShapeDtypeStruct((s, b, t, heads, hd), F32),
                   jax.ShapeDtypeStruct((b, t, n), BF16)],
        compiler_params=_params("parallel", "parallel", "arbitrary"),
    )(x, g.reshape(1, k), w)


def _mm_norm_res_kernel(a_ref, w_ref, g_ref, h_ref, o_ref):
    y = _dot(a_ref[...], w_ref[...])
    o_ref[...] = h_ref[...] + _rms(y, g_ref[...])


def matmul_norm_residual(a, w, g, h, tm=512):
    m, k = a.shape
    d = w.shape[1]
    tm = _tile(m, tm)
    return pl.pallas_call(
        _mm_norm_res_kernel, name="mm_norm_res",
        grid=(m // tm,),
        in_specs=[pl.BlockSpec((tm, k), lambda i: (i, 0)),
                  pl.BlockSpec((k, d), lambda i: (0, 0)),
                  pl.BlockSpec((1, d), lambda i: (0, 0)),
                  pl.BlockSpec((tm, d), lambda i: (i, 0))],
        out_specs=pl.BlockSpec((tm, d), lambda i: (i, 0)),
        out_shape=jax.ShapeDtypeStruct((m, d), F32),
        compiler_params=_params("parallel"),
    )(a, w, g.reshape(1, d), h)


def _ffn_kernel(h_ref, g1_ref, wu_ref, wd_ref, g2_ref, o_ref, u_ref, acc_ref):
    f = pl.program_id(1)

    @pl.when(f == 0)
    def _():
        u_ref[...] = _rms(h_ref[...], g1_ref[...]).astype(BF16)
        acc_ref[...] = jnp.zeros_like(acc_ref)

    a = _dot(u_ref[...], wu_ref[...])
    a = jnp.square(jnp.maximum(a, 0.0)).astype(BF16)
    acc_ref[...] += _dot(a, wd_ref[...])

    @pl.when(f == pl.num_programs(1) - 1)
    def _():
        o_ref[...] = h_ref[...] + _rms(acc_ref[...], g2_ref[...])


def ffn_block(h, g_pre, w_up, w_down, g_post, tm=512, tf=1024):
    m, d = h.shape
    ff = w_up.shape[1]
    tm, tf = _tile(m, tm), _tile(ff, tf, LANES)
    return pl.pallas_call(
        _ffn_kernel, name="ffn",
        grid=(m // tm, ff // tf),
        in_specs=[pl.BlockSpec((tm, d), lambda i, f: (i, 0)),
                  pl.BlockSpec((1, d), lambda i, f: (0, 0)),
                  pl.BlockSpec((d, tf), lambda i, f: (0, f)),
                  pl.BlockSpec((tf, d), lambda i, f: (f, 0)),
                  pl.BlockSpec((1, d), lambda i, f: (0, 0))],
        out_specs=pl.BlockSpec((tm, d), lambda i, f: (i, 0)),
        out_shape=jax.ShapeDtypeStruct((m, d), F32),
        scratch_shapes=[pltpu.VMEM((tm, d), BF16), pltpu.VMEM((tm, d), F32)],
        compiler_params=_params("parallel", "arbitrary"),
    )(h, g_pre.reshape(1, d), w_up, w_down, g_post.reshape(1, d))


def _mem_attn_kernel(h_ref, g1_ref, wq_ref, mk_ref, mv_ref, wo_ref, g2_ref, o_ref):
    h = h_ref[0]
    u = _rms(h, g1_ref[...]).astype(BF16)
    q = (_dot(u, wq_ref[...]) * (MEM_HD ** -0.5)).astype(BF16)
    heads = []
    for hh in range(MEM_HEADS):
        sl = slice(hh * MEM_HD, (hh + 1) * MEM_HD)
        s = _dot_nt(q[:, sl], mk_ref[0, :, hh, :].astype(BF16))
        p = jnp.exp(s - jnp.max(s, axis=-1, keepdims=True))
        o = _dot(p.astype(BF16), mv_ref[0, :, hh, :].astype(BF16)) / jnp.sum(p, axis=-1, keepdims=True)
        heads.append(o.astype(BF16))
    y = _dot(jnp.concatenate(heads, axis=-1), wo_ref[...])
    o_ref[0] = h + _rms(y, g2_ref[...])


def mem_attn_block(h, g_pre, w_q, mk, mv, w_o, g_post, tm=512):
    b, t, d = h.shape
    mt, mw = mk.shape[1], mk.shape[2] * mk.shape[3]
    tm = _tile(t, tm)
    mem_spec = pl.BlockSpec((1, mt) + mk.shape[2:], lambda bi, i: (bi, 0, 0, 0))
    return pl.pallas_call(
        _mem_attn_kernel, name="mem_attn",
        grid=(b, t // tm),
        in_specs=[pl.BlockSpec((1, tm, d), lambda bi, i: (bi, i, 0)),
                  pl.BlockSpec((1, d), lambda bi, i: (0, 0)),
                  pl.BlockSpec((d, mw), lambda bi, i: (0, 0)),
                  mem_spec, mem_spec,
                  pl.BlockSpec((mw, d), lambda bi, i: (0, 0)),
                  pl.BlockSpec((1, d), lambda bi, i: (0, 0))],
        out_specs=pl.BlockSpec((1, tm, d), lambda bi, i: (bi, i, 0)),
        out_shape=jax.ShapeDtypeStruct((b, t, d), F32),
        compiler_params=_params("parallel", "parallel"),
    )(h, g_pre.reshape(1, d), w_q, mk, mv, w_o, g_post.reshape(1, d))


def _sb_keys_block(qs, ks, vs, carries, mask, upper):
    tq = qs[0].shape[0]
    zs = [_dot_nt(q, k) * (SB_HD ** -0.5) for q, k in zip(qs, ks)]
    sps = [_softplus(z) for z in zs]
    keeps = [-sp if mask is None else jnp.where(mask, -sp, 0.0) for sp in sps]
    incl = _dot_split_lhs(jnp.concatenate(keeps, axis=0), upper)
    incls = [incl[n * tq:(n + 1) * tq] for n in range(len(qs))]
    ws = [jnp.exp(z - sp + (inc - keep) + c)
          for z, sp, inc, keep, c in zip(zs, sps, incls, keeps, carries)]
    if mask is not None:
        ws = [jnp.where(mask, w, 0.0) for w in ws]
    pvs = [_dot(w.astype(BF16), v) for w, v in zip(ws, vs)]
    return pvs, [c + inc[:, :1] for c, inc in zip(carries, incls)]


def _upper_ones(tk):
    return (_iota((tk, tk), 0) >= _iota((tk, tk), 1)).astype(BF16)


SB_DEAD_LOG = -104.0


def _sb_alive(visited, total, c_ref):
    return jnp.logical_and(visited < total, jnp.max(c_ref[...]) > SB_DEAD_LOG)


def _sb_prompt_kernel(q_ref, k_ref, v_ref, o_ref, acc_ref, c_ref, *, tq, tk, heads):
    i = pl.program_id(2)
    lanes = [slice(n * SB_HD, (n + 1) * SB_HD) for n in range(heads)]
    qs = [q_ref[0, :, ls] for ls in lanes]
    upper = _upper_ones(tk)
    acc_ref[...] = jnp.zeros_like(acc_ref)
    c_ref[...] = jnp.zeros_like(c_ref)
    nd = tq // tk
    qpos = _iota((tq, tk), 0)
    kcol = _iota((tq, tk), 1)

    def visit(start, mask):
        ks = [k_ref[0, pl.ds(start, tk), ls] for ls in lanes]
        vs = [v_ref[0, pl.ds(start, tk), ls] for ls in lanes]
        pvs, cs = _sb_keys_block(qs, ks, vs, [c_ref[n] for n in range(heads)], mask, upper)
        for n, ls in enumerate(lanes):
            acc_ref[:, ls] += pvs[n]
            c_ref[n] = cs[n]

    for dd in reversed(range(nd)):
        visit(pl.multiple_of(i * tq + dd * tk, tk), (kcol + dd * tk) < qpos)

    def body(jj):
        visit(pl.multiple_of((i * nd - 1 - jj) * tk, tk), None)
        return jj + 1

    lax.while_loop(lambda jj: _sb_alive(jj, i * nd, c_ref), body, 0)
    o_ref[0] = acc_ref[...].astype(o_ref.dtype)


def sb_attention_prompt(q, kv, tq=128, tk=128, heads=4):
    b, t, hd = q.shape
    tq, tk = min(tq, t), min(tk, t)
    groups = hd // (heads * SB_HD)
    wide = heads * SB_HD
    return pl.pallas_call(
        functools.partial(_sb_prompt_kernel, tq=tq, tk=tk, heads=heads), name="sb_prompt",
        grid=(b, groups, t // tq),
        in_specs=[pl.BlockSpec((1, tq, wide), lambda bi, hi, i: (bi, i, hi)),
                  pl.BlockSpec((1, t, wide), lambda bi, hi, i: (bi, 0, hi)),
                  pl.BlockSpec((1, t, wide), lambda bi, hi, i: (bi, 0, groups + hi))],
        out_specs=pl.BlockSpec((1, tq, wide), lambda bi, hi, i: (bi, i, hi)),
        out_shape=jax.ShapeDtypeStruct((b, t, hd), BF16),
        scratch_shapes=[pltpu.VMEM((tq, wide), F32), pltpu.VMEM((heads, tq, 1), F32)],
        compiler_params=_params("parallel", "parallel", "arbitrary"),
    )(q, kv, kv)


def _sb_sample_kernel(q_ref, kn_ref, vn_ref, kc_ref, vc_ref, o_ref, acc_ref, c_ref, *, ts, tk, heads):
    j = pl.program_id(1)
    lanes = [slice(n * SB_HD, (n + 1) * SB_HD) for n in range(heads)]
    qs = [q_ref[0, :, ls] for ls in lanes]

    @pl.when(j == 0)
    def _():
        mask = _iota((ts, ts), 1) < _iota((ts, ts), 0)
        pvs, cs = _sb_keys_block(qs, [kn_ref[0, :, ls] for ls in lanes], [vn_ref[0, :, ls] for ls in lanes],
                                 [jnp.zeros((ts, 1), F32)] * heads, mask, _upper_ones(ts))
        for n, ls in enumerate(lanes):
            acc_ref[:, ls] = pvs[n]
            c_ref[n] = cs[n]

    @pl.when(jnp.max(c_ref[...]) > SB_DEAD_LOG)
    def _():
        ks = [kc_ref[0, :, n, :].astype(BF16) for n in range(heads)]
        vs = [vc_ref[0, :, n, :].astype(BF16) for n in range(heads)]
        pvs, cs = _sb_keys_block(qs, ks, vs, [c_ref[n] for n in range(heads)], None, _upper_ones(tk))
        for n, ls in enumerate(lanes):
            acc_ref[:, ls] += pvs[n]
            c_ref[n] = cs[n]

    @pl.when(j == pl.num_programs(1) - 1)
    def _():
        o_ref[0] = acc_ref[...].astype(o_ref.dtype)


def sb_attention_sample(q, kv_new, cache_k, cache_v, tk=256):
    b, ts, hd = q.shape
    past, h = cache_k.shape[1], cache_k.shape[2]
    tk = _tile(past, tk)
    nkb = past // tk
    cache_spec = pl.BlockSpec((1, tk, h, SB_HD), lambda bi, j: (bi, nkb - 1 - j, 0, 0))
    return pl.pallas_call(
        functools.partial(_sb_sample_kernel, ts=ts, tk=tk, heads=h), name="sb_sample",
        grid=(b, nkb),
        in_specs=[pl.BlockSpec((1, ts, hd), lambda bi, j: (bi, 0, 0)),
                  pl.BlockSpec((1, ts, hd), lambda bi, j: (bi, 0, 0)),
                  pl.BlockSpec((1, ts, hd), lambda bi, j: (bi, 0, 1)),
                  cache_spec, cache_spec],
        out_specs=pl.BlockSpec((1, ts, hd), lambda bi, j: (bi, 0, 0)),
        out_shape=jax.ShapeDtypeStruct((b, ts, hd), BF16),
        scratch_shapes=[pltpu.VMEM((ts, hd), F32), pltpu.VMEM((h, ts, 1), F32)],
        compiler_params=_params("parallel", "arbitrary"),
    )(q, kv_new, kv_new, cache_k, cache_v)


def _rw_proj_kernel(h_ref, halo_ref, shift_ref, g_ref, mu_ref, wr_ref, wk_ref, wv_ref,
                    w1_ref, w2_ref, a1_ref, a2_ref, g1_ref, g2_ref,
                    w0_ref, a0_ref, kk_ref, ka_ref, seg_ref,
                    r_out, lw_out, k_out, v_out, a_out, b_out, g_out, last_out,
                    xr_ref, xk_ref, xv_ref, hw_ref, ha_ref, hg_ref, *, tm):
    i, j = pl.program_id(1), pl.program_id(2)

    @pl.when(j == 0)
    def _():
        u = _rms(h_ref[0], g_ref[...])
        before = _rms(halo_ref[0], g_ref[...])[7:8]
        first = jnp.where(i == 0, shift_ref[0], before)
        prev = jnp.where(_iota((tm, 1), 0) == 0, first, pltpu.roll(u, 1, 0))
        xx = prev - u
        mu = mu_ref[...]
        xr_ref[...] = (u + xx * mu[0:1]).astype(BF16)
        xw = (u + xx * mu[1:2]).astype(BF16)
        xk_ref[...] = (u + xx * mu[2:3]).astype(BF16)
        xv_ref[...] = (u + xx * mu[3:4]).astype(BF16)
        xa = (u + xx * mu[4:5]).astype(BF16)
        xg = (u + xx * mu[5:6]).astype(BF16)
        hw_ref[...] = jnp.tanh(_dot(xw, w1_ref[...])).astype(BF16)
        ha_ref[...] = _dot(xa, a1_ref[...]).astype(BF16)
        hg_ref[...] = jax.nn.sigmoid(_dot(xg, g1_ref[...])).astype(BF16)

        @pl.when(i == pl.num_programs(1) - 1)
        def _():
            last_out[0] = u[tm - 1:tm]

    r = _dot(xr_ref[...], wr_ref[...])
    k = _dot(xk_ref[...], wk_ref[...])
    v = _dot(xv_ref[...], wv_ref[...])
    w_log = -_softplus(-(w0_ref[...] + _dot(hw_ref[...], w2_ref[...]))) - 0.5
    a = jax.nn.sigmoid(a0_ref[...] + _dot(ha_ref[...], a2_ref[...]))
    gate = _dot(hg_ref[...], g2_ref[...])
    kk = k * kk_ref[...]
    ss = _dot_split_lhs(kk * kk, seg_ref[...])
    kk = kk * lax.rsqrt(jnp.maximum(ss, 1e-24))
    r_out[0] = r.astype(BF16)
    lw_out[0] = -jnp.exp(w_log)
    k_out[0] = (k * (1.0 + (a - 1.0) * ka_ref[...])).astype(BF16)
    v_out[0] = v.astype(BF16)
    a_out[0] = (-kk).astype(BF16)
    b_out[0] = (kk * a).astype(BF16)
    g_out[0] = gate.astype(BF16)


def _segment_ones(n, seg):
    idx = jnp.arange(n) // seg
    return (idx[:, None] == idx[None, :]).astype(BF16)


def rwkv_project(h, shift0, g, p, tm=512, tn=256):
    b, t, d = h.shape
    tm, tn = _tile(t, tm), _tile(d, tn, LANES)
    lp, gl = RW_LORA_PAD, p["g1"].shape[1]
    tok = lambda bi, i, j: (bi, i, j)
    col = lambda bi, i, j: (0, j)
    fix = lambda bi, i, j: (0, 0)
    tok_spec = pl.BlockSpec((1, tm, tn), tok)
    outs = pl.pallas_call(
        functools.partial(_rw_proj_kernel, tm=tm), name="rw_proj",
        grid=(b, t // tm, d // tn),
        in_specs=[pl.BlockSpec((1, tm, d), lambda bi, i, j: (bi, i, 0)),
                  pl.BlockSpec((1, 8, d), lambda bi, i, j: (bi, jnp.maximum(i * (tm // 8) - 1, 0), 0)),
                  pl.BlockSpec((1, 1, d), lambda bi, i, j: (bi, 0, 0)),
                  pl.BlockSpec((1, d), fix),
                  pl.BlockSpec((6, d), fix),
                  pl.BlockSpec((d, tn), col), pl.BlockSpec((d, tn), col), pl.BlockSpec((d, tn), col),
                  pl.BlockSpec((d, lp), fix), pl.BlockSpec((lp, tn), col),
                  pl.BlockSpec((d, lp), fix), pl.BlockSpec((lp, tn), col),
                  pl.BlockSpec((d, gl), fix), pl.BlockSpec((gl, tn), col),
                  pl.BlockSpec((1, tn), col), pl.BlockSpec((1, tn), col),
                  pl.BlockSpec((1, tn), col), pl.BlockSpec((1, tn), col),
                  pl.BlockSpec((tn, tn), fix)],
        out_specs=[tok_spec] * 7 + [pl.BlockSpec((1, 1, d), lambda bi, i, j: (bi, 0, 0))],
        out_shape=[jax.ShapeDtypeStruct((b, t, d), BF16), jax.ShapeDtypeStruct((b, t, d), F32)]
        + [jax.ShapeDtypeStruct((b, t, d), BF16)] * 5 + [jax.ShapeDtypeStruct((b, 1, d), F32)],
        scratch_shapes=[pltpu.VMEM((tm, d), BF16)] * 3
        + [pltpu.VMEM((tm, lp), BF16), pltpu.VMEM((tm, lp), BF16), pltpu.VMEM((tm, gl), BF16)],
        compiler_params=_params("parallel", "arbitrary", "arbitrary"),
    )(h, h, shift0.reshape(b, 1, d), g.reshape(1, d), p["mu"], p["w_r"], p["w_k"], p["w_v"],
      p["w1"], p["w2"], p["a1"], p["a2"], p["g1"], p["g2"],
      p["w0"].reshape(1, d), p["a0"].reshape(1, d), p["k_k"].reshape(1, d), p["k_a"].reshape(1, d),
      _segment_ones(tn, RW_HD))
    return outs


def _wkv_kernel(r_ref, lw_ref, k_ref, v_ref, a_ref, b_ref, g_ref, s0_ref, lnw_ref, lnb_ref, rk_ref,
                seg_ref, y_out, s_out, s_ref, *, L, pairs, solve_parts):
    c = pl.program_id(2)

    @pl.when(c == 0)
    def _():
        s_ref[...] = s0_ref[0]

    L2 = 2 * L
    rr, cc = _iota((L2, L2), 0), _iota((L2, L2), 1)
    same_head = (rr // L) == (cc // L)
    strict = same_head & ((rr % L) > (cc % L))
    incl = same_head & ((rr % L) >= (cc % L))
    eye = (rr == cc).astype(F32)
    tri = (_iota((L, L), 0) >= _iota((L, L), 1)).astype(BF16)
    head0 = _iota((1, LANES), 1) < RW_HD
    blockdiag = (_iota((LANES, LANES), 0) // RW_HD) == (_iota((LANES, LANES), 1) // RW_HD)
    seg = seg_ref[...]
    rows2 = max(L2, LANES)
    steps = int(math.log2(L)) - 1
    P = range(pairs)
    lanes = [slice(pp * LANES, (pp + 1) * LANES) for pp in P]

    def stack(x):
        return jnp.concatenate([jnp.where(head0, x, 0.0), jnp.where(head0, 0.0, x)], axis=0)

    def unstack(x2):
        return x2[:L] + x2[L:]

    lw = [lw_ref[0, :, ls] for ls in lanes]
    cum = [_dot_split_rhs(tri, x, 3) for x in lw]
    p_in = [jnp.exp(x) for x in cum]
    p_ex = [jnp.exp(x - y) for x, y in zip(cum, lw)]
    p_inv = [jnp.exp(-x) for x in cum]
    p_end = [x[L - 1:L] for x in p_in]
    r = [r_ref[0, :, ls].astype(F32) for ls in lanes]
    k = [k_ref[0, :, ls].astype(F32) for ls in lanes]
    v = [v_ref[0, :, ls].astype(F32) for ls in lanes]
    at = [a_ref[0, :, ls].astype(F32) * x for ls, x in zip(lanes, p_ex)]
    bt = [b_ref[0, :, ls].astype(F32) * x for ls, x in zip(lanes, p_inv)]
    kt = [x * y for x, y in zip(k, p_inv)]
    rt = [x * y for x, y in zip(r, p_in)]
    lhs = [jnp.concatenate([stack(x), stack(y)], axis=0).astype(BF16) for x, y in zip(at, rt)]
    rhs = [jnp.concatenate([stack(x), stack(y)], axis=0).astype(BF16) for x, y in zip(bt, kt)]
    gram = [_dot_nt(x, y) for x, y in zip(lhs, rhs)]
    s_old = [s_ref[pp] for pp in P]
    on_s = [_dot_nt(x, y.astype(BF16)) for x, y in zip(lhs, s_old)]
    v2 = [stack(x).astype(BF16) for x in v]
    n_ab = [jnp.where(strict, x[:L2, :L2], 0.0) for x in gram]
    n_ak = [jnp.where(strict, x[:L2, L2:], 0.0).astype(BF16) for x in gram]
    w2 = [x[:L2] + _dot(y, z) for x, y, z in zip(on_s, n_ak, v2)]
    t_acc, pw = [eye + x for x in n_ab], n_ab
    for _ in range(steps):
        pw = [_solve_dot(x, x, solve_parts) for x in pw]
        t_acc = [x + _solve_dot(x, y, solve_parts) for x, y in zip(t_acc, pw)]
    u2 = [_solve_dot(x, y, solve_parts) for x, y in zip(t_acc, w2)]
    m_rb = [jnp.where(incl, x[L2:, :L2], 0.0).astype(BF16) for x in gram]
    m_rk = [jnp.where(incl, x[L2:, L2:], 0.0).astype(BF16) for x in gram]
    y2 = [x[L2:] + _dot(a_, b_.astype(BF16)) + _dot(c_, d_)
          for x, a_, b_, c_, d_ in zip(on_s, m_rb, u2, m_rk, v2)]
    u = [unstack(x) for x in u2]
    y = [unstack(x) for x in y2]
    x_keys = [_pad_rows(jnp.concatenate([b_ * e_, k_ * e_], axis=0), rows2).astype(BF16)
              for b_, k_, e_ in zip(bt, kt, p_end)]
    z_vals = [_pad_rows(jnp.concatenate([u_, v_], axis=0), rows2) for u_, v_ in zip(u, v)]
    upd = [_dot(z_.T.astype(BF16), x_) for z_, x_ in zip(z_vals, x_keys)]
    for pp in P:
        s_new = s_old[pp] * p_end[pp] + jnp.where(blockdiag, upd[pp], 0.0)
        s_ref[pp] = s_new
        s_out[0, pp] = s_new

    mean = [_dot_split_lhs(x, seg) * (1.0 / RW_HD) for x in y]
    dev = [x - m_ for x, m_ in zip(y, mean)]
    var = [_dot_split_lhs(x * x, seg) * (1.0 / RW_HD) for x in dev]
    bonus = [_dot_split_lhs(r_ * k_ * rk_ref[:, ls], seg) * v_ for r_, k_, v_, ls in zip(r, k, v, lanes)]
    for pp, ls in enumerate(lanes):
        yn = dev[pp] * lax.rsqrt(var[pp] + RW_GN_EPS) * lnw_ref[:, ls] + lnb_ref[:, ls]
        y_out[0, :, ls] = ((yn + bonus[pp]) * g_ref[0, :, ls].astype(F32)).astype(BF16)


def _solve_dot(a, b, parts):
    if parts == 1:
        return _dot(a.astype(BF16), b.astype(BF16))
    return _dot_hi(a, b)


def wkv_scan(r, lw, k, v, a, b, gate, s0, ln_w, ln_b, r_k, chunk, pairs=4, solve_parts=1):
    bsz, t, d = r.shape
    L = min(chunk, t)
    npairs = d // LANES
    pairs = min(pairs, npairs)
    w = pairs * LANES
    tok = pl.BlockSpec((1, L, w), lambda bi, pi, c: (bi, c, pi))
    vec = pl.BlockSpec((1, w), lambda bi, pi, c: (0, pi))
    st = pl.BlockSpec((1, pairs, LANES, LANES), lambda bi, pi, c: (bi, pi, 0, 0))
    y, s_fin = pl.pallas_call(
        functools.partial(_wkv_kernel, L=L, pairs=pairs, solve_parts=solve_parts), name="wkv",
        grid=(bsz, npairs // pairs, t // L),
        in_specs=[tok] * 7 + [st, vec, vec, vec, pl.BlockSpec((LANES, LANES), lambda bi, pi, c: (0, 0))],
        out_specs=[tok, st],
        out_shape=[jax.ShapeDtypeStruct((bsz, t, d), BF16),
                   jax.ShapeDtypeStruct((bsz, npairs, LANES, LANES), F32)],
        scratch_shapes=[pltpu.VMEM((pairs, LANES, LANES), F32)],
        compiler_params=_params("parallel", "parallel", "arbitrary"),
    )(r, lw, k, v, a, b, gate, s0, ln_w.reshape(1, d), ln_b.reshape(1, d), r_k.reshape(1, d),
      _segment_ones(LANES, RW_HD))
    return y, s_fin


def _pair_state(s):
    b, h, n, _ = s.shape
    s = s.reshape(b, h // 2, 2, n, n)
    z = jnp.zeros_like(s[:, :, 0])
    top = jnp.concatenate([s[:, :, 0], z], axis=-1)
    bot = jnp.concatenate([z, s[:, :, 1]], axis=-1)
    return jnp.concatenate([top, bot], axis=-2)


def _unpair_state(sp):
    n = RW_HD
    b, hp = sp.shape[:2]
    return jnp.stack([sp[:, :, :n, :n], sp[:, :, n:, n:]], axis=2).reshape(b, 2 * hp, n, n)


def _mla_kv_kernel(h_ref, g_ref, w_ref, gkv_ref, cos_ref, sin_ref, ckv_out, kr_out, kcat_out):
    u = _rms(h_ref[0], g_ref[...]).astype(BF16)
    kv = _dot(u, w_ref[...])
    r = MLA_KV_RANK
    c = _rms(kv[:, :r], gkv_ref[...])
    kr2 = kv[:, r:r + LANES] * cos_ref[...] + kv[:, r + LANES:r + 2 * LANES] * sin_ref[...]
    ckv_out[0] = c
    kr_out[0] = kr2[:, :MLA_ROPE]
    kcat_out[0] = jnp.concatenate([c, kr2], axis=-1).astype(BF16)


def mla_kv(h, g, w_ext, g_kv, cos2, sin2, tm=512):
    b, t, d = h.shape
    tm = _tile(t, tm)
    n = w_ext.shape[1]
    return pl.pallas_call(
        _mla_kv_kernel, name="mla_kv",
        grid=(b, t // tm),
        in_specs=[pl.BlockSpec((1, tm, d), lambda bi, i: (bi, i, 0)),
                  pl.BlockSpec((1, d), lambda bi, i: (0, 0)),
                  pl.BlockSpec((d, n), lambda bi, i: (0, 0)),
                  pl.BlockSpec((1, MLA_KV_RANK), lambda bi, i: (0, 0)),
                  pl.BlockSpec((tm, LANES), lambda bi, i: (i, 0)),
                  pl.BlockSpec((tm, LANES), lambda bi, i: (i, 0))],
        out_specs=[pl.BlockSpec((1, tm, MLA_KV_RANK), lambda bi, i: (bi, i, 0)),
                   pl.BlockSpec((1, tm, MLA_ROPE), lambda bi, i: (bi, i, 0)),
                   pl.BlockSpec((1, tm, MLA_KCAT), lambda bi, i: (bi, i, 0))],
        out_shape=[jax.ShapeDtypeStruct((b, t, MLA_KV_RANK), F32),
                   jax.ShapeDtypeStruct((b, t, MLA_ROPE), F32),
                   jax.ShapeDtypeStruct((b, t, MLA_KCAT), BF16)],
        compiler_params=_params("parallel", "parallel"),
    )(h, g.reshape(1, d), w_ext, g_kv.reshape(1, MLA_KV_RANK), cos2, sin2)


def _mla_q_kernel(nope_ref, rope_ref, rot_ref, cos_ref, sin_ref, wuk_ref, o_ref):
    hh = pl.program_id(2)
    lat = _dot(nope_ref[0], wuk_ref[0]) * MLA_SCALE
    rope = (rope_ref[0].astype(F32) * cos_ref[...] + rot_ref[0].astype(F32) * sin_ref[...]) * MLA_SCALE
    mine = (_iota((1, LANES), 1) // MLA_ROPE) == (hh % 2)
    rope = jnp.where(mine, rope, 0.0)
    o_ref[0, 0] = jnp.concatenate([lat, rope], axis=-1).astype(BF16)


def mla_q_prepare(q_all, w_uk_t, cos2, sin2, tq=512):
    b, t, _ = q_all.shape
    tq = _tile(t, tq)
    h = MLA_HEADS
    nb = h * MLA_NOPE // LANES
    rb = h * MLA_ROPE // LANES
    return pl.pallas_call(
        _mla_q_kernel, name="mla_q",
        grid=(b, t // tq, h),
        in_specs=[pl.BlockSpec((1, tq, LANES), lambda bi, i, hh: (bi, i, hh)),
                  pl.BlockSpec((1, tq, LANES), lambda bi, i, hh: (bi, i, nb + hh // 2)),
                  pl.BlockSpec((1, tq, LANES), lambda bi, i, hh: (bi, i, nb + rb + hh // 2)),
                  pl.BlockSpec((tq, LANES), lambda bi, i, hh: (i, 0)),
                  pl.BlockSpec((tq, LANES), lambda bi, i, hh: (i, 0)),
                  pl.BlockSpec((1, MLA_NOPE, MLA_KV_RANK), lambda bi, i, hh: (hh, 0, 0))],
        out_specs=pl.BlockSpec((1, 1, tq, MLA_KCAT), lambda bi, i, hh: (bi, hh, i, 0)),
        out_shape=jax.ShapeDtypeStruct((b, h, t, MLA_KCAT), BF16),
        compiler_params=_params("parallel", "parallel", "arbitrary"),
    )(q_all, q_all, q_all, cos2, sin2, w_uk_t)


def _mla_attn_kernel(q_ref, k_ref, wuv_ref, o_ref, acc_ref, m_ref, l_ref, *, tq, tk, q0, tail, causal_blocks):
    i = pl.program_id(1)
    h = MLA_HEADS
    rows = h * tq
    q = q_ref[0].reshape(rows, MLA_KCAT)
    acc_ref[...] = jnp.zeros_like(acc_ref)
    m_ref[...] = jnp.full_like(m_ref, -jnp.inf)
    l_ref[...] = jnp.zeros_like(l_ref)
    qchunk = (q0 + i * tq + _iota((rows, 1), 0) % tq) // CHUNK

    def visit(k, start, masked):
        n = k.shape[0]
        s = _dot_nt(q, k)
        if masked:
            kchunk = (start + _iota((1, n), 1)) // CHUNK
            s = jnp.where(kchunk <= qchunk, s, -jnp.inf)
        m_old = m_ref[...]
        m_new = jnp.maximum(m_old, jnp.max(s, axis=-1, keepdims=True))
        alpha = jnp.exp(m_old - m_new)
        p = jnp.exp(s - m_new)
        l_ref[...] = alpha * l_ref[...] + jnp.sum(p, axis=-1, keepdims=True)
        acc_ref[...] = alpha * acc_ref[...] + _dot(p.astype(BF16), k[:, :MLA_KV_RANK])
        m_ref[...] = m_new

    def body(masked, j, carry):
        start = pl.multiple_of(j * tk, tk)
        visit(k_ref[0, pl.ds(start, tk), :], start, masked)
        return carry

    klen = k_ref.shape[1]
    nblk = (klen - tail) // tk
    if causal_blocks:
        nblk = jnp.minimum(((i + 1) * tq + tk - 1) // tk, nblk)
    first_chunk = (q0 + i * tq) // CHUNK
    n_full = jnp.minimum(((first_chunk + 1) * CHUNK) // tk, nblk)
    lax.fori_loop(0, n_full, functools.partial(body, False), 0)
    lax.fori_loop(n_full, nblk, functools.partial(body, True), 0)
    if tail:
        start = klen - tail
        visit(k_ref[0, start:, :], start, True)
    o_lat = (acc_ref[...] / l_ref[...]).astype(BF16)
    outs = [_dot(o_lat[hh * tq:(hh + 1) * tq], wuv_ref[hh]).astype(BF16) for hh in range(h)]
    o_ref[0] = jnp.concatenate(outs, axis=-1)


def mla_attention(q_full, kcat, w_uv, q0, causal_blocks, tq=128, tk=512):
    b, h, t, _ = q_full.shape
    klen = kcat.shape[1]
    tq = _tile(t, tq, 16)
    tk = min(tk, klen)
    tail = klen % tk
    if causal_blocks:
        assert tail == 0 and tk % CHUNK == 0 and q0 == 0 and klen == t
    return pl.pallas_call(
        functools.partial(_mla_attn_kernel, tq=tq, tk=tk, q0=q0, tail=tail, causal_blocks=causal_blocks),
        name="mla_attn",
        grid=(b, t // tq),
        in_specs=[pl.BlockSpec((1, h, tq, MLA_KCAT), lambda bi, i: (bi, 0, i, 0)),
                  pl.BlockSpec((1, klen, MLA_KCAT), lambda bi, i: (bi, 0, 0)),
                  pl.BlockSpec((h, MLA_KV_RANK, MLA_VD), lambda bi, i: (0, 0, 0))],
        out_specs=pl.BlockSpec((1, tq, h * MLA_VD), lambda bi, i: (bi, i, 0)),
        out_shape=jax.ShapeDtypeStruct((b, t, h * MLA_VD), BF16),
        scratch_shapes=[pltpu.VMEM((h * tq, MLA_KV_RANK), F32), pltpu.VMEM((h * tq, 1), F32),
                        pltpu.VMEM((h * tq, 1), F32)],
        compiler_params=_params("parallel", "arbitrary"),
    )(q_full, kcat, w_uv)


def _rope_tables(pos):
    half = MLA_ROPE // 2
    inv_freq = ROPE_THETA ** (-jnp.arange(half, dtype=F32) / half)
    ang = pos.astype(F32)[:, None] * inv_freq[None, :]
    cos, sin = jnp.cos(ang), jnp.sin(ang)
    return jnp.tile(cos, (1, 4)), jnp.tile(sin, (1, 4))


def _rot_cols(w):
    half = MLA_ROPE // 2
    return jnp.concatenate([-w[..., half:], w[..., :half]], axis=-1)


def _mlstm_kernel(q_ref, k_ref, v_ref, og_ref, gate_ref, bias_ref, c0_ref, n0_ref, m0_ref, gh_ref,
                  h_out, c_out, n_out, m_out, c_ref, n_ref, m_ref, *, L, heads):
    ci = pl.program_id(1)

    @pl.when(ci == 0)
    def _():
        c_ref[...] = c0_ref[0]
        n_ref[...] = n0_ref[0]
        m_ref[...] = m0_ref[0]

    tt, ss = _iota((L, L), 0), _iota((L, L), 1)
    causal = tt >= ss
    tri = causal.astype(BF16)
    scale = ML_DQK ** -0.5
    rows = max(L, LANES)

    gates = gate_ref[0] + bias_ref[...]
    log_f = -_softplus(-gates)
    cum_all = _dot_split_rhs(tri, log_f, 3)
    gates_t = _pad_rows(gates, rows).T
    cum_t = _pad_rows(cum_all, rows).T

    for hh in range(heads):
        q = q_ref[0, :, hh * ML_DQK:(hh + 1) * ML_DQK]
        k = k_ref[0, :, hh * ML_DQK:(hh + 1) * ML_DQK]
        v = v_ref[0, :, hh * ML_DV:(hh + 1) * ML_DV]
        i_c = gates[:, hh:hh + 1]
        cum_c = cum_all[:, heads + hh:heads + hh + 1]
        i_r = gates_t[hh:hh + 1, :L]
        cum_r = cum_t[heads + hh:heads + hh + 1, :L]
        m_old = m_ref[hh][:, :1]
        c_old, n_old = c_ref[hh], n_ref[hh]

        dmat = jnp.where(causal, cum_c - cum_r + i_r, -jnp.inf)
        inter = cum_c + m_old
        m_t = jnp.maximum(jnp.max(dmat, axis=1, keepdims=True), inter)
        s = _dot_nt(q, k) * scale * jnp.exp(dmat - m_t)
        inter_w = jnp.exp(inter - m_t) * scale
        num = _dot(s.astype(BF16), v) + inter_w * _dot(q, c_old.astype(BF16))
        den = jnp.sum(s, axis=1, keepdims=True) + inter_w * jnp.sum(q.astype(F32) * n_old, axis=1, keepdims=True)
        out = num / jnp.maximum(jnp.abs(den), jnp.exp(-m_t))

        last = cum_c[L - 1:L]
        src = last - cum_c + i_c
        m_new = jnp.maximum(last + m_old, jnp.max(src, axis=0, keepdims=True))
        keep = jnp.exp(last + m_old - m_new)
        kw = k.astype(F32) * jnp.exp(src - m_new)
        upd = _dot(_pad_rows(kw, rows).T.astype(BF16), _pad_rows(v, rows))
        c_new = keep * c_old + upd
        n_new = keep * n_old + jnp.sum(kw, axis=0, keepdims=True)
        c_ref[hh], n_ref[hh] = c_new, n_new
        m_ref[hh] = jnp.broadcast_to(m_new, (1, LANES))
        c_out[0, hh], n_out[0, hh] = c_new, n_new
        m_out[0, hh] = jnp.broadcast_to(m_new, (1, LANES))

        hn = _rms(out, gh_ref[:, hh * ML_DV:(hh + 1) * ML_DV])
        gate = jax.nn.sigmoid(og_ref[0, :, hh * ML_DV:(hh + 1) * ML_DV].astype(F32))
        h_out[0, :, hh * ML_DV:(hh + 1) * ML_DV] = (hn * gate).astype(BF16)


def mlstm_scan(proj, gates, bias, c0, n0, m0, g_h, chunk):
    b, t, _ = proj.shape
    h = ML_HEADS
    L = min(chunk, t)
    qw, vw = h * ML_DQK, h * ML_DV
    st = lambda shape: pl.BlockSpec((1,) + shape, lambda bi, c: (bi, 0, 0, 0))
    outs = pl.pallas_call(
        functools.partial(_mlstm_kernel, L=L, heads=h), name="mlstm",
        grid=(b, t // L),
        in_specs=[pl.BlockSpec((1, L, qw), lambda bi, c: (bi, c, 0)),
                  pl.BlockSpec((1, L, qw), lambda bi, c: (bi, c, 1)),
                  pl.BlockSpec((1, L, vw), lambda bi, c: (bi, c, 2 * qw // vw)),
                  pl.BlockSpec((1, L, vw), lambda bi, c: (bi, c, 2 * qw // vw + 1)),
                  pl.BlockSpec((1, L, LANES), lambda bi, c: (bi, c, 0)),
                  pl.BlockSpec((1, LANES), lambda bi, c: (0, 0)),
                  st((h, ML_DQK, ML_DV)), st((h, 1, ML_DQK)), st((h, 1, LANES)),
                  pl.BlockSpec((1, vw), lambda bi, c: (0, 0))],
        out_specs=[pl.BlockSpec((1, L, vw), lambda bi, c: (bi, c, 0)),
                   st((h, ML_DQK, ML_DV)), st((h, 1, ML_DQK)), st((h, 1, LANES))],
        out_shape=[jax.ShapeDtypeStruct((b, t, vw), BF16),
                   jax.ShapeDtypeStruct((b, h, ML_DQK, ML_DV), F32),
                   jax.ShapeDtypeStruct((b, h, 1, ML_DQK), F32),
                   jax.ShapeDtypeStruct((b, h, 1, LANES), F32)],
        scratch_shapes=[pltpu.VMEM((h, ML_DQK, ML_DV), F32), pltpu.VMEM((h, 1, ML_DQK), F32),
                        pltpu.VMEM((h, 1, LANES), F32)],
        compiler_params=_params("parallel", "arbitrary"),
    )(proj, proj, proj, proj, gates, bias, c0, n0, m0, g_h.reshape(1, vw))
    return outs


def _flat(x):
    return x.reshape(-1, x.shape[-1])


def _sb_layer(hp, hs, g, w_qkv, w_o, cache_k, cache_v):
    d = hp.shape[-1]
    w_q, w_kv = w_qkv[:, :d], w_qkv[:, d:]
    res = []
    for h, caches in ((hp, None), (hs, (cache_k, cache_v))):
        b, t, _ = h.shape
        (q,) = norm_matmul(_flat(h), g[0], w_q, [BF16])
        kv32, kv16 = norm_matmul_heads(h, g[0], w_kv, SB_HEADS)
        q = q.reshape(b, t, d)
        if caches is None:
            o = sb_attention_prompt(q, kv16)
        else:
            o = sb_attention_sample(q, kv16, caches[0], caches[1])
        h_new = matmul_norm_residual(_flat(o), w_o, g[1], _flat(h)).reshape(b, t, d)
        res.append((h_new, kv32[0], kv32[1]))
    return res


def _rw_layer(h, g, shift0, wkv0, p, w_o, ln_w, ln_b, r_k, chunk):
    b, t, d = h.shape
    r, lw, k, v, a, bb, gate, last = rwkv_project(h, shift0, g[0], p)
    y, s_fin = wkv_scan(r, lw, k, v, a, bb, gate, _pair_state(wkv0), ln_w, ln_b, r_k, chunk)
    h_new = matmul_norm_residual(_flat(y), w_o, g[1], _flat(h)).reshape(b, t, d)
    return h_new, last.reshape(b, d), _unpair_state(s_fin)


def _mla_layer(h, g, pos0, kv_cache, w):
    b, t, d = h.shape
    cos2, sin2 = _rope_tables(pos0 + jnp.arange(t))
    ckv, krope, kcat = mla_kv(h, g[0], w["dkv_ext"], w["g_kv"], cos2, sin2)
    (qd,) = norm_matmul(_flat(h), g[0], w["dq"], [BF16])
    (q_all,) = norm_matmul(qd, w["g_q"], w["uq_ext"], [BF16])
    q_full = mla_q_prepare(q_all.reshape(b, t, -1), w["uk_t"], cos2, sin2)
    if kv_cache is None:
        o = mla_attention(q_full, kcat, w["uv"], 0, True)
    else:
        c_ckv, c_kr = kv_cache
        kr16 = c_kr.astype(BF16)
        k_all = jnp.concatenate([jnp.concatenate([c_ckv.astype(BF16), kr16, kr16], axis=-1), kcat], axis=1)
        o = mla_attention(q_full, k_all, w["uv"], pos0, False)
    h_new = matmul_norm_residual(_flat(o), w["o"], g[1], _flat(h)).reshape(b, t, d)
    return h_new, ckv, krope


def _ml_layer(h, g, c0, n0, m0, chunk, w):
    b, t, d = h.shape
    (proj,) = norm_matmul(_flat(h), g[0], w["in"], [BF16])
    (gates,) = norm_matmul(_flat(h), g[0], w["if_pad"], [F32])
    hm, c, n, m = mlstm_scan(proj.reshape(b, t, -1), gates.reshape(b, t, LANES), w["b_pad"],
                             c0, n0.reshape(b, ML_HEADS, 1, ML_DQK),
                             jnp.broadcast_to(m0[:, :, None, None], (b, ML_HEADS, 1, LANES)), w["g_h"], chunk)
    h_new = matmul_norm_residual(_flat(hm), w["out"], g[1], _flat(h)).reshape(b, t, d)
    return h_new, c, n.reshape(b, ML_HEADS, ML_DQK), m[:, :, 0, 0]


def kernel(x_prompt, x_sample, mem_prompt, cache_sb_k, cache_sb_v, state_rwkv_shift, state_rwkv_wkv, cache_mla_ckv, cache_mla_krope, state_mlstm_c, state_mlstm_n, state_mlstm_m, cache_mem_k, cache_mem_v, norm_g, sb_w_qkv, sb_w_o, rw_mu, rw_w_r, rw_w_k, rw_w_v, rw_w_o, rw_w0, rw_w1, rw_w2, rw_a0, rw_a1, rw_a2, rw_g1, rw_g2, rw_k_k, rw_k_a, rw_r_k, rw_ln_w, rw_ln_b, mla_w_dq, mla_g_q, mla_w_uq, mla_w_dkv, mla_g_kv, mla_w_uk, mla_w_uv, mla_w_o, ml_w_in, ml_w_if, ml_b_if, ml_g_h, ml_w_out, w_mem_q, w_mem_kv, w_mem_o, w_ffn_up, w_ffn_down):
    bp, seq_p, d = x_prompt.shape
    bs, t_s, _ = x_sample.shape
    past = cache_sb_k.shape[1]
    depth = norm_g.shape[0]
    bf = lambda w: w.astype(BF16)
    pad_cols = lambda w, n: jnp.pad(w, ((0, 0), (0, n - w.shape[1])))
    pad_rows = lambda w, n: jnp.pad(w, ((0, n - w.shape[0]), (0, 0)))

    rw_p = dict(mu=rw_mu, w_r=bf(rw_w_r), w_k=bf(rw_w_k), w_v=bf(rw_w_v),
                w1=bf(pad_cols(rw_w1, RW_LORA_PAD)), w2=bf(pad_rows(rw_w2, RW_LORA_PAD)),
                a1=bf(pad_cols(rw_a1, RW_LORA_PAD)), a2=bf(pad_rows(rw_a2, RW_LORA_PAD)),
                g1=bf(rw_g1), g2=bf(rw_g2), w0=rw_w0, a0=rw_a0, k_k=rw_k_k, k_a=rw_k_a)

    kr_w = mla_w_dkv[:, MLA_KV_RANK:]
    uq = mla_w_uq.reshape(-1, MLA_HEADS, MLA_NOPE + MLA_ROPE)
    uq_rope = uq[:, :, MLA_NOPE:]
    rank_q = uq.shape[0]
    mla_w = dict(
        dkv_ext=bf(jnp.concatenate([mla_w_dkv[:, :MLA_KV_RANK], kr_w, kr_w, _rot_cols(kr_w), _rot_cols(kr_w)], axis=1)),
        g_kv=mla_g_kv, dq=bf(mla_w_dq), g_q=mla_g_q,
        uq_ext=bf(jnp.concatenate([uq[:, :, :MLA_NOPE].reshape(rank_q, -1), uq_rope.reshape(rank_q, -1),
                                   _rot_cols(uq_rope).reshape(rank_q, -1)], axis=1)),
        uk_t=bf(jnp.transpose(mla_w_uk, (1, 2, 0))),
        uv=bf(jnp.transpose(mla_w_uv, (1, 0, 2))),
        o=bf(mla_w_o))

    ml_w = {"in": bf(ml_w_in), "if_pad": bf(pad_cols(ml_w_if, LANES)),
            "b_pad": jnp.pad(ml_b_if, (0, LANES - ml_b_if.shape[0])).reshape(1, LANES),
            "g_h": ml_g_h, "out": bf(ml_w_out)}

    hp, hs = x_prompt, x_sample
    mem_k_list, mem_v_list = [], []
    outs = {}
    for layer in range(depth):
        g = norm_g[layer]
        kind = layer % 4
        if kind == 0:
            (hp, kp, vp), (hs, ks, vs) = _sb_layer(hp, hs, g, bf(sb_w_qkv), bf(sb_w_o), cache_sb_k, cache_sb_v)
            outs["sb"] = (kp, vp, ks, vs)
        elif kind == 1:
            n_h = d // RW_HD
            hp, sh_p, wkv_p = _rw_layer(hp, g, jnp.zeros((bp, d), F32), jnp.zeros((bp, n_h, RW_HD, RW_HD), F32),
                                        rw_p, bf(rw_w_o), rw_ln_w, rw_ln_b, rw_r_k, CHUNK)
            hs, sh_s, wkv_s = _rw_layer(hs, g, state_rwkv_shift, state_rwkv_wkv,
                                        rw_p, bf(rw_w_o), rw_ln_w, rw_ln_b, rw_r_k, CHUNK)
            outs["rw"] = (sh_p, wkv_p, sh_s, wkv_s)
        elif kind == 2:
            hp, ckv_p, kr_p = _mla_layer(hp, g, 0, None, mla_w)
            hs, ckv_s, kr_s = _mla_layer(hs, g, past, (cache_mla_ckv, cache_mla_krope), mla_w)
            outs["mla"] = (ckv_p, kr_p, ckv_s, kr_s)
        else:
            zc = jnp.zeros((bp, ML_HEADS, ML_DQK, ML_DV), F32)
            hp, c_p, n_p, m_p = _ml_layer(hp, g, zc, jnp.zeros((bp, ML_HEADS, ML_DQK), F32),
                                          jnp.zeros((bp, ML_HEADS), F32), CHUNK, ml_w)
            hs, c_s, n_s, m_s = _ml_layer(hs, g, state_mlstm_c, state_mlstm_n, state_mlstm_m, t_s, ml_w)
            outs["ml"] = (c_p, n_p, m_p, c_s, n_s, m_s)

        mkv, _ = norm_matmul_heads(mem_prompt, None, bf(w_mem_kv[layer]), MEM_HEADS)
        mk, mv = mkv[0], mkv[1]
        mem_k_list.append(mk)
        mem_v_list.append(mv)
        wq, wo = bf(w_mem_q[layer]), bf(w_mem_o[layer])
        hp = mem_attn_block(hp, g[2], wq, mk, mv, wo, g[3])
        hs = mem_attn_block(hs, g[2], wq, cache_mem_k[layer], cache_mem_v[layer], wo, g[3])

        wu, wd = bf(w_ffn_up[layer]), bf(w_ffn_down[layer])
        hp = ffn_block(_flat(hp), g[4], wu, wd, g[5]).reshape(bp, seq_p, d)
        hs = ffn_block(_flat(hs), g[4], wu, wd, g[5]).reshape(bs, t_s, d)

    return (hp, hs) + outs["sb"] + outs["rw"] + outs["mla"] + outs["ml"] + (
        jnp.stack(mem_k_list, axis=0), jnp.stack(mem_v_list, axis=0))
```

```python
import functools
import math

import jax
import jax.numpy as jnp
from jax import lax
from jax.experimental import pallas as pl
from jax.experimental.pallas import tpu as pltpu

F32, BF16 = jnp.float32, jnp.bfloat16

CHUNK = 64
NORM_EPS = 1e-6
SB_HEADS, SB_HD = 16, 128
RW_HD = 64
RW_LORA_PAD = 128
RW_GN_EPS = 64e-5
MLA_HEADS, MLA_KV_RANK, MLA_NOPE, MLA_ROPE, MLA_VD = 16, 512, 128, 64, 128
MLA_SCALE = (MLA_NOPE + MLA_ROPE) ** -0.5
MLA_KCAT = MLA_KV_RANK + 2 * MLA_ROPE
ROPE_THETA = 10000.0
ML_HEADS, ML_DQK, ML_DV = 8, 128, 256
MEM_HEADS, MEM_HD = 4, 128

V7X_VMEM_LIMIT_BYTES = 56 * 1024 * 1024
LANES = 128

NT_DIMS = (((1,), (1,)), ((), ()))


def _params(*sem):
    return pltpu.CompilerParams(dimension_semantics=sem, vmem_limit_bytes=V7X_VMEM_LIMIT_BYTES)


def _tile(n, pref, align=8):
    if n <= pref:
        return n
    for t in range(pref, 0, -1):
        if n % t == 0 and t % align == 0:
            return t
    return n


def _rms(x, g, eps=NORM_EPS):
    return x * lax.rsqrt(jnp.mean(x * x, axis=-1, keepdims=True) + eps) * g


def _dot(a, b):
    return jnp.dot(a, b, preferred_element_type=F32)


def _dot_nt(a, b):
    return lax.dot_general(a, b, NT_DIMS, preferred_element_type=F32)


def _split(x, parts):
    out = []
    for _ in range(parts):
        p = x.astype(BF16)
        out.append(p)
        x = x - p.astype(F32)
    return out


def _dot_split_lhs(x, b_bf16, parts=2):
    acc = None
    for p in _split(x, parts):
        t = _dot(p, b_bf16)
        acc = t if acc is None else acc + t
    return acc


def _dot_split_rhs(a_bf16, x, parts=2):
    acc = None
    for p in _split(x, parts):
        t = _dot(a_bf16, p)
        acc = t if acc is None else acc + t
    return acc


def _dot_hi(a, b):
    ah, al = _split(a, 2)
    bh, bl = _split(b, 2)
    return _dot(ah, bh) + (_dot(ah, bl) + _dot(al, bh))


def _softplus(z):
    return jnp.maximum(z, 0.0) + jnp.log1p(jnp.exp(-jnp.abs(z)))


def _iota(shape, dim):
    return lax.broadcasted_iota(jnp.int32, shape, dim)


def _pad_rows(x, rows):
    if x.shape[0] >= rows:
        return x
    return jnp.concatenate([x, jnp.zeros((rows - x.shape[0], x.shape[1]), x.dtype)], axis=0)


def _norm_mm_kernel(x_ref, g_ref, w_ref, *refs, normed, n_out):
    outs, u_ref = refs[:n_out], refs[n_out]

    @pl.when(pl.program_id(1) == 0)
    def _():
        x = x_ref[...].astype(F32)
        if normed:
            x = _rms(x, g_ref[...])
        u_ref[...] = x.astype(BF16)

    y = _dot(u_ref[...], w_ref[...])
    for o in outs:
        o[...] = y.astype(o.dtype)


def norm_matmul(x, g, w, out_dtypes, normed=True, tm=512, tn=512):
    m, k = x.shape
    n = w.shape[1]
    tm, tn = _tile(m, tm), _tile(n, tn, LANES)
    if g is None:
        g = jnp.ones((k,), F32)
    outs = pl.pallas_call(
        functools.partial(_norm_mm_kernel, normed=normed, n_out=len(out_dtypes)), name="norm_mm",
        grid=(m // tm, n // tn),
        in_specs=[pl.BlockSpec((tm, k), lambda i, j: (i, 0)),
                  pl.BlockSpec((1, k), lambda i, j: (0, 0)),
                  pl.BlockSpec((k, tn), lambda i, j: (0, j))],
        out_specs=[pl.BlockSpec((tm, tn), lambda i, j: (i, j)) for _ in out_dtypes],
        out_shape=[jax.ShapeDtypeStruct((m, n), d) for d in out_dtypes],
        scratch_shapes=[pltpu.VMEM((tm, k), BF16)],
        compiler_params=_params("parallel", "arbitrary"),
    )(x, g.reshape(1, k), w)
    return outs


def _norm_mm_heads_kernel(x_ref, g_ref, w_ref, o4_ref, o16_ref, *, heads, hd, normed):
    x = x_ref[0]
    u = (_rms(x, g_ref[...]) if normed else x).astype(BF16)
    y = _dot(u, w_ref[...])
    o16_ref[0] = y.astype(BF16)
    for hh in range(heads):
        o4_ref[0, 0, :, hh, :] = y[:, hh * hd:(hh + 1) * hd]


def norm_matmul_heads(x, g, w, heads, tm=256):
    b, t, k = x.shape
    n = w.shape[1]
    hd = SB_HD
    wide = heads * hd
    s = n // wide
    tm = _tile(t, tm)
    normed = g is not None
    if not normed:
        g = jnp.ones((k,), F32)
    return pl.pallas_call(
        functools.partial(_norm_mm_heads_kernel, heads=heads, hd=hd, normed=normed), name="norm_mm_heads",
        grid=(b, t // tm, s),
        in_specs=[pl.BlockSpec((1, tm, k), lambda bi, i, j: (bi, i, 0)),
                  pl.BlockSpec((1, k), lambda bi, i, j: (0, 0)),
                  pl.BlockSpec((k, wide), lambda bi, i, j: (0, j))],
        out_specs=[pl.BlockSpec((1, 1, tm, heads, hd), lambda bi, i, j: (j, bi, i, 0, 0)),
                   pl.BlockSpec((1, tm, wide), lambda bi, i, j: (bi, i, j))],
        out_shape=[jax.ShapeDtypeStruct((s, b, t, heads, hd), F32),
                   jax.ShapeDtypeStruct((b, t, n), BF16)],
        compiler_params=_params("parallel", "parallel", "arbitrary"),
    )(x, g.reshape(1, k), w)


def _mm_norm_res_kernel(a_ref, w_ref, g_ref, h_ref, o_ref):
    y = _dot(a_ref[...], w_ref[...])
    o_ref[...] = h_ref[...] + _rms(y, g_ref[...])


def matmul_norm_residual(a, w, g, h, tm=512):
    m, k = a.shape
    d = w.shape[1]
    tm = _tile(m, tm)
    return pl.pallas_call(
        _mm_norm_res_kernel, name="mm_norm_res",
        grid=(m // tm,),
        in_specs=[pl.BlockSpec((tm, k), lambda i: (i, 0)),
                  pl.BlockSpec((k, d), lambda i: (0, 0)),
                  pl.BlockSpec((1, d), lambda i: (0, 0)),
                  pl.BlockSpec((tm, d), lambda i: (i, 0))],
        out_specs=pl.BlockSpec((tm, d), lambda i: (i, 0)),
        out_shape=jax.ShapeDtypeStruct((m, d), F32),
        compiler_params=_params("parallel"),
    )(a, w, g.reshape(1, d), h)


def _ffn_kernel(h_ref, g1_ref, wu_ref, wd_ref, g2_ref, o_ref, u_ref, acc_ref):
    f = pl.program_id(1)

    @pl.when(f == 0)
    def _():
        u_ref[...] = _rms(h_ref[...], g1_ref[...]).astype(BF16)
        acc_ref[...] = jnp.zeros_like(acc_ref)

    a = _dot(u_ref[...], wu_ref[...])
    a = jnp.square(jnp.maximum(a, 0.0)).astype(BF16)
    acc_ref[...] += _dot(a, wd_ref[...])

    @pl.when(f == pl.num_programs(1) - 1)
    def _():
        o_ref[...] = h_ref[...] + _rms(acc_ref[...], g2_ref[...])


def ffn_block(h, g_pre, w_up, w_down, g_post, tm=512, tf=1024):
    m, d = h.shape
    ff = w_up.shape[1]
    tm, tf = _tile(m, tm), _tile(ff, tf, LANES)
    return pl.pallas_call(
        _ffn_kernel, name="ffn",
        grid=(m // tm, ff // tf),
        in_specs=[pl.BlockSpec((tm, d), lambda i, f: (i, 0)),
                  pl.BlockSpec((1, d), lambda i, f: (0, 0)),
                  pl.BlockSpec((d, tf), lambda i, f: (0, f)),
                  pl.BlockSpec((tf, d), lambda i, f: (f, 0)),
                  pl.BlockSpec((1, d), lambda i, f: (0, 0))],
        out_specs=pl.BlockSpec((tm, d), lambda i, f: (i, 0)),
        out_shape=jax.ShapeDtypeStruct((m, d), F32),
        scratch_shapes=[pltpu.VMEM((tm, d), BF16), pltpu.VMEM((tm, d), F32)],
        compiler_params=_params("parallel", "arbitrary"),
    )(h, g_pre.reshape(1, d), w_up, w_down, g_post.reshape(1, d))


def _mem_attn_kernel(h_ref, g1_ref, wq_ref, mk_ref, mv_ref, wo_ref, g2_ref, o_ref):
    h = h_ref[0]
    u = _rms(h, g1_ref[...]).astype(BF16)
    q = (_dot(u, wq_ref[...]) * (MEM_HD ** -0.5)).astype(BF16)
    heads = []
    for hh in range(MEM_HEADS):
        sl = slice(hh * MEM_HD, (hh + 1) * MEM_HD)
        s = _dot_nt(q[:, sl], mk_ref[0, 0, :, hh, :].astype(BF16))
        p = jnp.exp(s - jnp.max(s, axis=-1, keepdims=True))
        o = _dot(p.astype(BF16), mv_ref[0, 0, :, hh, :].astype(BF16)) / jnp.sum(p, axis=-1, keepdims=True)
        heads.append(o.astype(BF16))
    y = _dot(jnp.concatenate(heads, axis=-1), wo_ref[...])
    o_ref[0] = h + _rms(y, g2_ref[...])


def mem_attn_block(h, g_pre, w_q, mk, k_at, mv, v_at, w_o, g_post, tm=512):
    b, t, d = h.shape
    mt, mw = mk.shape[2], mk.shape[3] * mk.shape[4]
    tm = _tile(t, tm)
    mem_spec = lambda at: pl.BlockSpec((1, 1, mt) + mk.shape[3:], lambda bi, i: (at, bi, 0, 0, 0))
    return pl.pallas_call(
        _mem_attn_kernel, name="mem_attn",
        grid=(b, t // tm),
        in_specs=[pl.BlockSpec((1, tm, d), lambda bi, i: (bi, i, 0)),
                  pl.BlockSpec((1, d), lambda bi, i: (0, 0)),
                  pl.BlockSpec((d, mw), lambda bi, i: (0, 0)),
                  mem_spec(k_at), mem_spec(v_at),
                  pl.BlockSpec((mw, d), lambda bi, i: (0, 0)),
                  pl.BlockSpec((1, d), lambda bi, i: (0, 0))],
        out_specs=pl.BlockSpec((1, tm, d), lambda bi, i: (bi, i, 0)),
        out_shape=jax.ShapeDtypeStruct((b, t, d), F32),
        compiler_params=_params("parallel", "parallel"),
    )(h, g_pre.reshape(1, d), w_q, mk, mv, w_o, g_post.reshape(1, d))


def _sb_keys_block(qs, ks, vs, carries, mask, upper):
    tq = qs[0].shape[0]
    zs = [_dot_nt(q, k) * (SB_HD ** -0.5) for q, k in zip(qs, ks)]
    sps = [_softplus(z) for z in zs]
    keeps = [-sp if mask is None else jnp.where(mask, -sp, 0.0) for sp in sps]
    incl = _dot_split_lhs(jnp.concatenate(keeps, axis=0), upper)
    incls = [incl[n * tq:(n + 1) * tq] for n in range(len(qs))]
    ws = [jnp.exp(z - sp + (inc - keep) + c)
          for z, sp, inc, keep, c in zip(zs, sps, incls, keeps, carries)]
    if mask is not None:
        ws = [jnp.where(mask, w, 0.0) for w in ws]
    pvs = [_dot(w.astype(BF16), v) for w, v in zip(ws, vs)]
    return pvs, [c + inc[:, :1] for c, inc in zip(carries, incls)]


def _upper_ones(tk):
    return (_iota((tk, tk), 0) >= _iota((tk, tk), 1)).astype(BF16)


SB_DEAD_LOG = -104.0


def _sb_alive(visited, total, c_ref):
    return jnp.logical_and(visited < total, jnp.max(c_ref[...]) > SB_DEAD_LOG)


def _sb_prompt_kernel(q_ref, k_ref, v_ref, o_ref, acc_ref, c_ref, *, tq, tk, heads):
    i = pl.program_id(2)
    lanes = [slice(n * SB_HD, (n + 1) * SB_HD) for n in range(heads)]
    qs = [q_ref[0, :, ls] for ls in lanes]
    upper = _upper_ones(tk)
    acc_ref[...] = jnp.zeros_like(acc_ref)
    c_ref[...] = jnp.zeros_like(c_ref)
    nd = tq // tk
    qpos = _iota((tq, tk), 0)
    kcol = _iota((tq, tk), 1)

    def visit(start, mask):
        ks = [k_ref[0, pl.ds(start, tk), ls] for ls in lanes]
        vs = [v_ref[0, pl.ds(start, tk), ls] for ls in lanes]
        pvs, cs = _sb_keys_block(qs, ks, vs, [c_ref[n] for n in range(heads)], mask, upper)
        for n, ls in enumerate(lanes):
            acc_ref[:, ls] += pvs[n]
            c_ref[n] = cs[n]

    for dd in reversed(range(nd)):
        visit(pl.multiple_of(i * tq + dd * tk, tk), (kcol + dd * tk) < qpos)

    def body(jj):
        visit(pl.multiple_of((i * nd - 1 - jj) * tk, tk), None)
        return jj + 1

    lax.while_loop(lambda jj: _sb_alive(jj, i * nd, c_ref), body, 0)
    o_ref[0] = acc_ref[...].astype(o_ref.dtype)


def sb_attention_prompt(q, kv, tq=128, tk=128, heads=4):
    b, t, hd = q.shape
    tq, tk = min(tq, t), min(tk, t)
    groups = hd // (heads * SB_HD)
    wide = heads * SB_HD
    return pl.pallas_call(
        functools.partial(_sb_prompt_kernel, tq=tq, tk=tk, heads=heads), name="sb_prompt",
        grid=(b, groups, t // tq),
        in_specs=[pl.BlockSpec((1, tq, wide), lambda bi, hi, i: (bi, i, hi)),
                  pl.BlockSpec((1, t, wide), lambda bi, hi, i: (bi, 0, hi)),
                  pl.BlockSpec((1, t, wide), lambda bi, hi, i: (bi, 0, groups + hi))],
        out_specs=pl.BlockSpec((1, tq, wide), lambda bi, hi, i: (bi, i, hi)),
        out_shape=jax.ShapeDtypeStruct((b, t, hd), BF16),
        scratch_shapes=[pltpu.VMEM((tq, wide), F32), pltpu.VMEM((heads, tq, 1), F32)],
        compiler_params=_params("parallel", "parallel", "arbitrary"),
    )(q, kv, kv)


def _sb_sample_kernel(q_ref, kn_ref, vn_ref, kc_ref, vc_ref, o_ref, acc_ref, c_ref, *, ts, tk, heads):
    j = pl.program_id(1)
    lanes = [slice(n * SB_HD, (n + 1) * SB_HD) for n in range(heads)]
    qs = [q_ref[0, :, ls] for ls in lanes]

    @pl.when(j == 0)
    def _():
        mask = _iota((ts, ts), 1) < _iota((ts, ts), 0)
        pvs, cs = _sb_keys_block(qs, [kn_ref[0, :, ls] for ls in lanes], [vn_ref[0, :, ls] for ls in lanes],
                                 [jnp.zeros((ts, 1), F32)] * heads, mask, _upper_ones(ts))
        for n, ls in enumerate(lanes):
            acc_ref[:, ls] = pvs[n]
            c_ref[n] = cs[n]

    @pl.when(jnp.max(c_ref[...]) > SB_DEAD_LOG)
    def _():
        ks = [kc_ref[0, :, n, :].astype(BF16) for n in range(heads)]
        vs = [vc_ref[0, :, n, :].astype(BF16) for n in range(heads)]
        pvs, cs = _sb_keys_block(qs, ks, vs, [c_ref[n] for n in range(heads)], None, _upper_ones(tk))
        for n, ls in enumerate(lanes):
            acc_ref[:, ls] += pvs[n]
            c_ref[n] = cs[n]

    @pl.when(j == pl.num_programs(1) - 1)
    def _():
        o_ref[0] = acc_ref[...].astype(o_ref.dtype)


def sb_attention_sample(q, kv_new, cache_k, cache_v, tk=256):
    b, ts, hd = q.shape
    past, h = cache_k.shape[1], cache_k.shape[2]
    tk = _tile(past, tk)
    nkb = past // tk
    cache_spec = pl.BlockSpec((1, tk, h, SB_HD), lambda bi, j: (bi, nkb - 1 - j, 0, 0))
    return pl.pallas_call(
        functools.partial(_sb_sample_kernel, ts=ts, tk=tk, heads=h), name="sb_sample",
        grid=(b, nkb),
        in_specs=[pl.BlockSpec((1, ts, hd), lambda bi, j: (bi, 0, 0)),
                  pl.BlockSpec((1, ts, hd), lambda bi, j: (bi, 0, 0)),
                  pl.BlockSpec((1, ts, hd), lambda bi, j: (bi, 0, 1)),
                  cache_spec, cache_spec],
        out_specs=pl.BlockSpec((1, ts, hd), lambda bi, j: (bi, 0, 0)),
        out_shape=jax.ShapeDtypeStruct((b, ts, hd), BF16),
        scratch_shapes=[pltpu.VMEM((ts, hd), F32), pltpu.VMEM((h, ts, 1), F32)],
        compiler_params=_params("parallel", "arbitrary"),
    )(q, kv_new, kv_new, cache_k, cache_v)


def _rw_proj_kernel(h_ref, halo_ref, shift_ref, g_ref, mu_ref, wr_ref, wk_ref, wv_ref,
                    w1_ref, w2_ref, a1_ref, a2_ref, g1_ref, g2_ref,
                    w0_ref, a0_ref, kk_ref, ka_ref, seg_ref,
                    r_out, lw_out, k_out, v_out, a_out, b_out, g_out, last_out,
                    xr_ref, xk_ref, xv_ref, hw_ref, ha_ref, hg_ref, *, tm):
    i, j = pl.program_id(1), pl.program_id(2)

    @pl.when(j == 0)
    def _():
        u = _rms(h_ref[0], g_ref[...])
        before = _rms(halo_ref[0], g_ref[...])[7:8]
        first = jnp.where(i == 0, shift_ref[0], before)
        prev = jnp.where(_iota((tm, 1), 0) == 0, first, pltpu.roll(u, 1, 0))
        xx = prev - u
        mu = mu_ref[...]
        xr_ref[...] = (u + xx * mu[0:1]).astype(BF16)
        xw = (u + xx * mu[1:2]).astype(BF16)
        xk_ref[...] = (u + xx * mu[2:3]).astype(BF16)
        xv_ref[...] = (u + xx * mu[3:4]).astype(BF16)
        xa = (u + xx * mu[4:5]).astype(BF16)
        xg = (u + xx * mu[5:6]).astype(BF16)
        hw_ref[...] = jnp.tanh(_dot(xw, w1_ref[...])).astype(BF16)
        ha_ref[...] = _dot(xa, a1_ref[...]).astype(BF16)
        hg_ref[...] = jax.nn.sigmoid(_dot(xg, g1_ref[...])).astype(BF16)

        @pl.when(i == pl.num_programs(1) - 1)
        def _():
            last_out[0] = u[tm - 1:tm]

    r = _dot(xr_ref[...], wr_ref[...])
    k = _dot(xk_ref[...], wk_ref[...])
    v = _dot(xv_ref[...], wv_ref[...])
    w_log = -_softplus(-(w0_ref[...] + _dot(hw_ref[...], w2_ref[...]))) - 0.5
    a = jax.nn.sigmoid(a0_ref[...] + _dot(ha_ref[...], a2_ref[...]))
    gate = _dot(hg_ref[...], g2_ref[...])
    kk = k * kk_ref[...]
    ss = _dot_split_lhs(kk * kk, seg_ref[...])
    kk = kk * lax.rsqrt(jnp.maximum(ss, 1e-24))
    r_out[0] = r.astype(BF16)
    lw_out[0] = -jnp.exp(w_log)
    k_out[0] = (k * (1.0 + (a - 1.0) * ka_ref[...])).astype(BF16)
    v_out[0] = v.astype(BF16)
    a_out[0] = (-kk).astype(BF16)
    b_out[0] = (kk * a).astype(BF16)
    g_out[0] = gate.astype(BF16)


def _segment_ones(n, seg):
    idx = jnp.arange(n) // seg
    return (idx[:, None] == idx[None, :]).astype(BF16)


def rwkv_project(h, shift0, g, p, tm=512, tn=256):
    b, t, d = h.shape
    tm, tn = _tile(t, tm), _tile(d, tn, LANES)
    lp, gl = RW_LORA_PAD, p["g1"].shape[1]
    tok = lambda bi, i, j: (bi, i, j)
    col = lambda bi, i, j: (0, j)
    fix = lambda bi, i, j: (0, 0)
    tok_spec = pl.BlockSpec((1, tm, tn), tok)
    outs = pl.pallas_call(
        functools.partial(_rw_proj_kernel, tm=tm), name="rw_proj",
        grid=(b, t // tm, d // tn),
        in_specs=[pl.BlockSpec((1, tm, d), lambda bi, i, j: (bi, i, 0)),
                  pl.BlockSpec((1, 8, d), lambda bi, i, j: (bi, jnp.maximum(i * (tm // 8) - 1, 0), 0)),
                  pl.BlockSpec((1, 1, d), lambda bi, i, j: (bi, 0, 0)),
                  pl.BlockSpec((1, d), fix),
                  pl.BlockSpec((6, d), fix),
                  pl.BlockSpec((d, tn), col), pl.BlockSpec((d, tn), col), pl.BlockSpec((d, tn), col),
                  pl.BlockSpec((d, lp), fix), pl.BlockSpec((lp, tn), col),
                  pl.BlockSpec((d, lp), fix), pl.BlockSpec((lp, tn), col),
                  pl.BlockSpec((d, gl), fix), pl.BlockSpec((gl, tn), col),
                  pl.BlockSpec((1, tn), col), pl.BlockSpec((1, tn), col),
                  pl.BlockSpec((1, tn), col), pl.BlockSpec((1, tn), col),
                  pl.BlockSpec((tn, tn), fix)],
        out_specs=[tok_spec] * 7 + [pl.BlockSpec((1, 1, d), lambda bi, i, j: (bi, 0, 0))],
        out_shape=[jax.ShapeDtypeStruct((b, t, d), BF16), jax.ShapeDtypeStruct((b, t, d), F32)]
        + [jax.ShapeDtypeStruct((b, t, d), BF16)] * 5 + [jax.ShapeDtypeStruct((b, 1, d), F32)],
        scratch_shapes=[pltpu.VMEM((tm, d), BF16)] * 3
        + [pltpu.VMEM((tm, lp), BF16), pltpu.VMEM((tm, lp), BF16), pltpu.VMEM((tm, gl), BF16)],
        compiler_params=_params("parallel", "arbitrary", "arbitrary"),
    )(h, h, shift0.reshape(b, 1, d), g.reshape(1, d), p["mu"], p["w_r"], p["w_k"], p["w_v"],
      p["w1"], p["w2"], p["a1"], p["a2"], p["g1"], p["g2"],
      p["w0"].reshape(1, d), p["a0"].reshape(1, d), p["k_k"].reshape(1, d), p["k_a"].reshape(1, d),
      _segment_ones(tn, RW_HD))
    return outs


def _wkv_kernel(r_ref, lw_ref, k_ref, v_ref, a_ref, b_ref, g_ref, s0_ref, lnw_ref, lnb_ref, rk_ref,
                seg_ref, y_out, s_out, s_ref, *, L, pairs, solve_parts):
    c = pl.program_id(2)

    @pl.when(c == 0)
    def _():
        s_ref[...] = s0_ref[0]

    L2 = 2 * L
    rr, cc = _iota((L2, L2), 0), _iota((L2, L2), 1)
    same_head = (rr // L) == (cc // L)
    strict = same_head & ((rr % L) > (cc % L))
    incl = same_head & ((rr % L) >= (cc % L))
    eye = (rr == cc).astype(F32)
    tri = (_iota((L, L), 0) >= _iota((L, L), 1)).astype(BF16)
    head0 = _iota((1, LANES), 1) < RW_HD
    blockdiag = (_iota((LANES, LANES), 0) // RW_HD) == (_iota((LANES, LANES), 1) // RW_HD)
    seg = seg_ref[...]
    rows2 = max(L2, LANES)
    steps = int(math.log2(L)) - 1
    P = range(pairs)
    lanes = [slice(pp * LANES, (pp + 1) * LANES) for pp in P]

    def stack(x):
        return jnp.concatenate([jnp.where(head0, x, 0.0), jnp.where(head0, 0.0, x)], axis=0)

    def unstack(x2):
        return x2[:L] + x2[L:]

    lw = [lw_ref[0, :, ls] for ls in lanes]
    cum = [_dot_split_rhs(tri, x, 3) for x in lw]
    p_in = [jnp.exp(x) for x in cum]
    p_ex = [jnp.exp(x - y) for x, y in zip(cum, lw)]
    p_inv = [jnp.exp(-x) for x in cum]
    p_end = [x[L - 1:L] for x in p_in]
    r = [r_ref[0, :, ls].astype(F32) for ls in lanes]
    k = [k_ref[0, :, ls].astype(F32) for ls in lanes]
    v = [v_ref[0, :, ls].astype(F32) for ls in lanes]
    at = [a_ref[0, :, ls].astype(F32) * x for ls, x in zip(lanes, p_ex)]
    bt = [b_ref[0, :, ls].astype(F32) * x for ls, x in zip(lanes, p_inv)]
    kt = [x * y for x, y in zip(k, p_inv)]
    rt = [x * y for x, y in zip(r, p_in)]
    lhs = [jnp.concatenate([stack(x), stack(y)], axis=0).astype(BF16) for x, y in zip(at, rt)]
    rhs = [jnp.concatenate([stack(x), stack(y)], axis=0).astype(BF16) for x, y in zip(bt, kt)]
    gram = [_dot_nt(x, y) for x, y in zip(lhs, rhs)]
    s_old = [s_ref[pp] for pp in P]
    on_s = [_dot_nt(x, y.astype(BF16)) for x, y in zip(lhs, s_old)]
    v2 = [stack(x).astype(BF16) for x in v]
    n_ab = [jnp.where(strict, x[:L2, :L2], 0.0) for x in gram]
    n_ak = [jnp.where(strict, x[:L2, L2:], 0.0).astype(BF16) for x in gram]
    w2 = [x[:L2] + _dot(y, z) for x, y, z in zip(on_s, n_ak, v2)]
    t_acc, pw = [eye + x for x in n_ab], n_ab
    for _ in range(steps):
        pw = [_solve_dot(x, x, solve_parts) for x in pw]
        t_acc = [x + _solve_dot(x, y, solve_parts) for x, y in zip(t_acc, pw)]
    u2 = [_solve_dot(x, y, solve_parts) for x, y in zip(t_acc, w2)]
    m_rb = [jnp.where(incl, x[L2:, :L2], 0.0).astype(BF16) for x in gram]
    m_rk = [jnp.where(incl, x[L2:, L2:], 0.0).astype(BF16) for x in gram]
    y2 = [x[L2:] + _dot(a_, b_.astype(BF16)) + _dot(c_, d_)
          for x, a_, b_, c_, d_ in zip(on_s, m_rb, u2, m_rk, v2)]
    u = [unstack(x) for x in u2]
    y = [unstack(x) for x in y2]
    x_keys = [_pad_rows(jnp.concatenate([b_ * e_, k_ * e_], axis=0), rows2).astype(BF16)
              for b_, k_, e_ in zip(bt, kt, p_end)]
    z_vals = [_pad_rows(jnp.concatenate([u_, v_], axis=0), rows2) for u_, v_ in zip(u, v)]
    upd = [_dot(z_.T.astype(BF16), x_) for z_, x_ in zip(z_vals, x_keys)]
    for pp in P:
        s_new = s_old[pp] * p_end[pp] + jnp.where(blockdiag, upd[pp], 0.0)
        s_ref[pp] = s_new
        s_out[0, pp] = s_new

    mean = [_dot_split_lhs(x, seg) * (1.0 / RW_HD) for x in y]
    dev = [x - m_ for x, m_ in zip(y, mean)]
    var = [_dot_split_lhs(x * x, seg) * (1.0 / RW_HD) for x in dev]
    bonus = [_dot_split_lhs(r_ * k_ * rk_ref[:, ls], seg) * v_ for r_, k_, v_, ls in zip(r, k, v, lanes)]
    for pp, ls in enumerate(lanes):
        yn = dev[pp] * lax.rsqrt(var[pp] + RW_GN_EPS) * lnw_ref[:, ls] + lnb_ref[:, ls]
        y_out[0, :, ls] = ((yn + bonus[pp]) * g_ref[0, :, ls].astype(F32)).astype(BF16)


def _solve_dot(a, b, parts):
    if parts == 1:
        return _dot(a.astype(BF16), b.astype(BF16))
    return _dot_hi(a, b)


def wkv_scan(r, lw, k, v, a, b, gate, s0, ln_w, ln_b, r_k, chunk, pairs=8, solve_parts=1):
    bsz, t, d = r.shape
    L = min(chunk, t)
    npairs = d // LANES
    pairs = min(pairs, npairs)
    w = pairs * LANES
    tok = pl.BlockSpec((1, L, w), lambda bi, pi, c: (bi, c, pi))
    vec = pl.BlockSpec((1, w), lambda bi, pi, c: (0, pi))
    st = pl.BlockSpec((1, pairs, LANES, LANES), lambda bi, pi, c: (bi, pi, 0, 0))
    y, s_fin = pl.pallas_call(
        functools.partial(_wkv_kernel, L=L, pairs=pairs, solve_parts=solve_parts), name="wkv",
        grid=(bsz, npairs // pairs, t // L),
        in_specs=[tok] * 7 + [st, vec, vec, vec, pl.BlockSpec((LANES, LANES), lambda bi, pi, c: (0, 0))],
        out_specs=[tok, st],
        out_shape=[jax.ShapeDtypeStruct((bsz, t, d), BF16),
                   jax.ShapeDtypeStruct((bsz, npairs, LANES, LANES), F32)],
        scratch_shapes=[pltpu.VMEM((pairs, LANES, LANES), F32)],
        compiler_params=_params("parallel", "parallel", "arbitrary"),
    )(r, lw, k, v, a, b, gate, s0, ln_w.reshape(1, d), ln_b.reshape(1, d), r_k.reshape(1, d),
      _segment_ones(LANES, RW_HD))
    return y, s_fin


def _pair_state(s):
    b, h, n, _ = s.shape
    s = s.reshape(b, h // 2, 2, n, n)
    z = jnp.zeros_like(s[:, :, 0])
    top = jnp.concatenate([s[:, :, 0], z], axis=-1)
    bot = jnp.concatenate([z, s[:, :, 1]], axis=-1)
    return jnp.concatenate([top, bot], axis=-2)


def _unpair_state(sp):
    n = RW_HD
    b, hp = sp.shape[:2]
    return jnp.stack([sp[:, :, :n, :n], sp[:, :, n:, n:]], axis=2).reshape(b, 2 * hp, n, n)


def _mla_kv_kernel(h_ref, g_ref, w_ref, gkv_ref, cos_ref, sin_ref, ckv_out, kr_out, kcat_out):
    u = _rms(h_ref[0], g_ref[...]).astype(BF16)
    kv = _dot(u, w_ref[...])
    r = MLA_KV_RANK
    c = _rms(kv[:, :r], gkv_ref[...])
    kr2 = kv[:, r:r + LANES] * cos_ref[...] + kv[:, r + LANES:r + 2 * LANES] * sin_ref[...]
    ckv_out[0] = c
    kr_out[0] = kr2[:, :MLA_ROPE]
    kcat_out[0] = jnp.concatenate([c, kr2], axis=-1).astype(BF16)


def mla_kv(h, g, w_ext, g_kv, cos2, sin2, tm=512):
    b, t, d = h.shape
    tm = _tile(t, tm)
    n = w_ext.shape[1]
    return pl.pallas_call(
        _mla_kv_kernel, name="mla_kv",
        grid=(b, t // tm),
        in_specs=[pl.BlockSpec((1, tm, d), lambda bi, i: (bi, i, 0)),
                  pl.BlockSpec((1, d), lambda bi, i: (0, 0)),
                  pl.BlockSpec((d, n), lambda bi, i: (0, 0)),
                  pl.BlockSpec((1, MLA_KV_RANK), lambda bi, i: (0, 0)),
                  pl.BlockSpec((tm, LANES), lambda bi, i: (i, 0)),
                  pl.BlockSpec((tm, LANES), lambda bi, i: (i, 0))],
        out_specs=[pl.BlockSpec((1, tm, MLA_KV_RANK), lambda bi, i: (bi, i, 0)),
                   pl.BlockSpec((1, tm, MLA_ROPE), lambda bi, i: (bi, i, 0)),
                   pl.BlockSpec((1, tm, MLA_KCAT), lambda bi, i: (bi, i, 0))],
        out_shape=[jax.ShapeDtypeStruct((b, t, MLA_KV_RANK), F32),
                   jax.ShapeDtypeStruct((b, t, MLA_ROPE), F32),
                   jax.ShapeDtypeStruct((b, t, MLA_KCAT), BF16)],
        compiler_params=_params("parallel", "parallel"),
    )(h, g.reshape(1, d), w_ext, g_kv.reshape(1, MLA_KV_RANK), cos2, sin2)


def _mla_q_kernel(nope_ref, rope_ref, rot_ref, cos_ref, sin_ref, wuk_ref, o_ref):
    hh = pl.program_id(2)
    lat = _dot(nope_ref[0], wuk_ref[0]) * MLA_SCALE
    rope = (rope_ref[0].astype(F32) * cos_ref[...] + rot_ref[0].astype(F32) * sin_ref[...]) * MLA_SCALE
    mine = (_iota((1, LANES), 1) // MLA_ROPE) == (hh % 2)
    rope = jnp.where(mine, rope, 0.0)
    o_ref[0, 0] = jnp.concatenate([lat, rope], axis=-1).astype(BF16)


def mla_q_prepare(q_all, w_uk_t, cos2, sin2, tq=512):
    b, t, _ = q_all.shape
    tq = _tile(t, tq)
    h = MLA_HEADS
    nb = h * MLA_NOPE // LANES
    rb = h * MLA_ROPE // LANES
    return pl.pallas_call(
        _mla_q_kernel, name="mla_q",
        grid=(b, t // tq, h),
        in_specs=[pl.BlockSpec((1, tq, LANES), lambda bi, i, hh: (bi, i, hh)),
                  pl.BlockSpec((1, tq, LANES), lambda bi, i, hh: (bi, i, nb + hh // 2)),
                  pl.BlockSpec((1, tq, LANES), lambda bi, i, hh: (bi, i, nb + rb + hh // 2)),
                  pl.BlockSpec((tq, LANES), lambda bi, i, hh: (i, 0)),
                  pl.BlockSpec((tq, LANES), lambda bi, i, hh: (i, 0)),
                  pl.BlockSpec((1, MLA_NOPE, MLA_KV_RANK), lambda bi, i, hh: (hh, 0, 0))],
        out_specs=pl.BlockSpec((1, 1, tq, MLA_KCAT), lambda bi, i, hh: (bi, hh, i, 0)),
        out_shape=jax.ShapeDtypeStruct((b, h, t, MLA_KCAT), BF16),
        compiler_params=_params("parallel", "parallel", "arbitrary"),
    )(q_all, q_all, q_all, cos2, sin2, w_uk_t)


def _mla_attn_kernel(q_ref, k_ref, wuv_ref, o_ref, acc_ref, m_ref, l_ref, *, tq, tk, q0, tail, causal_blocks,
                     groups):
    i = pl.program_id(1)
    h = MLA_HEADS
    hg = h // groups
    rows = hg * tq
    qs = [q_ref[0, g * hg:(g + 1) * hg].reshape(rows, MLA_KCAT) for g in range(groups)]
    acc_ref[...] = jnp.zeros_like(acc_ref)
    m_ref[...] = jnp.full_like(m_ref, -jnp.inf)
    l_ref[...] = jnp.zeros_like(l_ref)
    qchunk = (q0 + i * tq + _iota((rows, 1), 0) % tq) // CHUNK

    def visit(k, start, masked):
        n = k.shape[0]
        kv = k[:, :MLA_KV_RANK]
        visible = ((start + _iota((1, n), 1)) // CHUNK <= qchunk) if masked else None
        s_next = _dot_nt(qs[0], k)
        for g in range(groups):
            s = s_next
            if g + 1 < groups:
                s_next = _dot_nt(qs[g + 1], k)
            if masked:
                s = jnp.where(visible, s, -jnp.inf)
            rs = pl.ds(g * rows, rows)
            m_old = m_ref[rs]
            m_new = jnp.maximum(m_old, jnp.max(s, axis=-1, keepdims=True))
            alpha = jnp.exp(m_old - m_new)
            p = jnp.exp(s - m_new)
            l_ref[rs] = alpha * l_ref[rs] + jnp.sum(p, axis=-1, keepdims=True)
            acc_ref[rs] = alpha * acc_ref[rs] + _dot(p.astype(BF16), kv)
            m_ref[rs] = m_new

    def body(masked, j, carry):
        start = pl.multiple_of(j * tk, tk)
        visit(k_ref[0, pl.ds(start, tk), :], start, masked)
        return carry

    klen = k_ref.shape[1]
    nblk = (klen - tail) // tk
    if causal_blocks:
        nblk = jnp.minimum(((i + 1) * tq + tk - 1) // tk, nblk)
    first_chunk = (q0 + i * tq) // CHUNK
    n_full = jnp.minimum(((first_chunk + 1) * CHUNK) // tk, nblk)
    lax.fori_loop(0, n_full, functools.partial(body, False), 0)
    lax.fori_loop(n_full, nblk, functools.partial(body, True), 0)
    if tail:
        start = klen - tail
        visit(k_ref[0, start:, :], start, True)
    o_lat = (acc_ref[...] / l_ref[...]).astype(BF16)
    outs = [_dot(o_lat[hh * tq:(hh + 1) * tq], wuv_ref[hh]).astype(BF16) for hh in range(h)]
    o_ref[0] = jnp.concatenate(outs, axis=-1)


def mla_attention(q_full, kcat, w_uv, q0, causal_blocks, tq=128, tk=512):
    b, h, t, _ = q_full.shape
    klen = kcat.shape[1]
    tq = _tile(t, tq, 16)
    tk = min(tk, klen)
    tail = klen % tk
    if causal_blocks:
        assert tail == 0 and tk % CHUNK == 0 and q0 == 0 and klen == t
    return pl.pallas_call(
        functools.partial(_mla_attn_kernel, tq=tq, tk=tk, q0=q0, tail=tail, causal_blocks=causal_blocks,
                          groups=4),
        name="mla_attn",
        grid=(b, t // tq),
        in_specs=[pl.BlockSpec((1, h, tq, MLA_KCAT), lambda bi, i: (bi, 0, i, 0)),
                  pl.BlockSpec((1, klen, MLA_KCAT), lambda bi, i: (bi, 0, 0)),
                  pl.BlockSpec((h, MLA_KV_RANK, MLA_VD), lambda bi, i: (0, 0, 0))],
        out_specs=pl.BlockSpec((1, tq, h * MLA_VD), lambda bi, i: (bi, i, 0)),
        out_shape=jax.ShapeDtypeStruct((b, t, h * MLA_VD), BF16),
        scratch_shapes=[pltpu.VMEM((h * tq, MLA_KV_RANK), F32), pltpu.VMEM((h * tq, 1), F32),
                        pltpu.VMEM((h * tq, 1), F32)],
        compiler_params=_params("parallel", "arbitrary"),
    )(q_full, kcat, w_uv)


def _rope_tables(pos):
    half = MLA_ROPE // 2
    inv_freq = ROPE_THETA ** (-jnp.arange(half, dtype=F32) / half)
    ang = pos.astype(F32)[:, None] * inv_freq[None, :]
    cos, sin = jnp.cos(ang), jnp.sin(ang)
    return jnp.tile(cos, (1, 4)), jnp.tile(sin, (1, 4))


def _rot_cols(w):
    half = MLA_ROPE // 2
    return jnp.concatenate([-w[..., half:], w[..., :half]], axis=-1)


def _mlstm_kernel(q_ref, k_ref, v_ref, og_ref, gate_ref, bias_ref, c0_ref, n0_ref, m0_ref, gh_ref,
                  h_out, c_out, n_out, m_out, c_ref, n_ref, m_ref, *, L, heads):
    ci = pl.program_id(1)

    @pl.when(ci == 0)
    def _():
        c_ref[...] = c0_ref[0]
        n_ref[...] = n0_ref[0]
        m_ref[...] = m0_ref[0]

    tt, ss = _iota((L, L), 0), _iota((L, L), 1)
    causal = tt >= ss
    tri = causal.astype(BF16)
    scale = ML_DQK ** -0.5
    rows = max(L, LANES)

    gates = gate_ref[0] + bias_ref[...]
    log_f = -_softplus(-gates)
    cum_all = _dot_split_rhs(tri, log_f, 3)
    gates_t = _pad_rows(gates, rows).T
    cum_t = _pad_rows(cum_all, rows).T

    H = range(heads)
    qk_l = [slice(hh * ML_DQK, (hh + 1) * ML_DQK) for hh in H]
    v_l = [slice(hh * ML_DV, (hh + 1) * ML_DV) for hh in H]
    q = [q_ref[0, :, ls] for ls in qk_l]
    k = [k_ref[0, :, ls] for ls in qk_l]
    v = [v_ref[0, :, ls] for ls in v_l]
    i_c = [gates[:, hh:hh + 1] for hh in H]
    cum_c = [cum_all[:, heads + hh:heads + hh + 1] for hh in H]
    i_r = [gates_t[hh:hh + 1, :L] for hh in H]
    cum_r = [cum_t[heads + hh:heads + hh + 1, :L] for hh in H]
    m_old = [m_ref[hh][:, :1] for hh in H]
    c_old = [c_ref[hh] for hh in H]
    n_old = [n_ref[hh] for hh in H]

    qk = [_dot_nt(a_, b_) for a_, b_ in zip(q, k)]
    q_c = [_dot(a_, b_.astype(BF16)) for a_, b_ in zip(q, c_old)]
    q_n = [jnp.sum(a_.astype(F32) * b_, axis=1, keepdims=True) for a_, b_ in zip(q, n_old)]
    dmat = [jnp.where(causal, a_ - b_ + c_, -jnp.inf) for a_, b_, c_ in zip(cum_c, cum_r, i_r)]
    inter = [a_ + b_ for a_, b_ in zip(cum_c, m_old)]
    m_t = [jnp.maximum(jnp.max(a_, axis=1, keepdims=True), b_) for a_, b_ in zip(dmat, inter)]
    s = [a_ * scale * jnp.exp(b_ - c_) for a_, b_, c_ in zip(qk, dmat, m_t)]
    inter_w = [jnp.exp(a_ - b_) * scale for a_, b_ in zip(inter, m_t)]
    num = [_dot(a_.astype(BF16), b_) + c_ * d_ for a_, b_, c_, d_ in zip(s, v, inter_w, q_c)]
    den = [jnp.sum(a_, axis=1, keepdims=True) + b_ * c_ for a_, b_, c_ in zip(s, inter_w, q_n)]
    out = [a_ / jnp.maximum(jnp.abs(b_), jnp.exp(-c_)) for a_, b_, c_ in zip(num, den, m_t)]

    last = [a_[L - 1:L] for a_ in cum_c]
    src = [a_ - b_ + c_ for a_, b_, c_ in zip(last, cum_c, i_c)]
    m_new = [jnp.maximum(a_ + b_, jnp.max(c_, axis=0, keepdims=True)) for a_, b_, c_ in zip(last, m_old, src)]
    keep = [jnp.exp(a_ + b_ - c_) for a_, b_, c_ in zip(last, m_old, m_new)]
    kw = [a_.astype(F32) * jnp.exp(b_ - c_) for a_, b_, c_ in zip(k, src, m_new)]
    upd = [_dot(_pad_rows(a_, rows).T.astype(BF16), _pad_rows(b_, rows)) for a_, b_ in zip(kw, v)]
    for hh in H:
        c_new = keep[hh] * c_old[hh] + upd[hh]
        n_new = keep[hh] * n_old[hh] + jnp.sum(kw[hh], axis=0, keepdims=True)
        c_ref[hh], n_ref[hh] = c_new, n_new
        m_ref[hh] = jnp.broadcast_to(m_new[hh], (1, LANES))
        c_out[0, hh], n_out[0, hh] = c_new, n_new
        m_out[0, hh] = jnp.broadcast_to(m_new[hh], (1, LANES))
    for hh in H:
        hn = _rms(out[hh], gh_ref[:, v_l[hh]])
        gate = jax.nn.sigmoid(og_ref[0, :, v_l[hh]].astype(F32))
        h_out[0, :, v_l[hh]] = (hn * gate).astype(BF16)


def mlstm_scan(proj, gates, bias, c0, n0, m0, g_h, chunk):
    b, t, _ = proj.shape
    h = ML_HEADS
    L = min(chunk, t)
    qw, vw = h * ML_DQK, h * ML_DV
    st = lambda shape: pl.BlockSpec((1,) + shape, lambda bi, c: (bi, 0, 0, 0))
    outs = pl.pallas_call(
        functools.partial(_mlstm_kernel, L=L, heads=h), name="mlstm",
        grid=(b, t // L),
        in_specs=[pl.BlockSpec((1, L, qw), lambda bi, c: (bi, c, 0)),
                  pl.BlockSpec((1, L, qw), lambda bi, c: (bi, c, 1)),
                  pl.BlockSpec((1, L, vw), lambda bi, c: (bi, c, 2 * qw // vw)),
                  pl.BlockSpec((1, L, vw), lambda bi, c: (bi, c, 2 * qw // vw + 1)),
                  pl.BlockSpec((1, L, LANES), lambda bi, c: (bi, c, 0)),
                  pl.BlockSpec((1, LANES), lambda bi, c: (0, 0)),
                  st((h, ML_DQK, ML_DV)), st((h, 1, ML_DQK)), st((h, 1, LANES)),
                  pl.BlockSpec((1, vw), lambda bi, c: (0, 0))],
        out_specs=[pl.BlockSpec((1, L, vw), lambda bi, c: (bi, c, 0)),
                   st((h, ML_DQK, ML_DV)), st((h, 1, ML_DQK)), st((h, 1, LANES))],
        out_shape=[jax.ShapeDtypeStruct((b, t, vw), BF16),
                   jax.ShapeDtypeStruct((b, h, ML_DQK, ML_DV), F32),
                   jax.ShapeDtypeStruct((b, h, 1, ML_DQK), F32),
                   jax.ShapeDtypeStruct((b, h, 1, LANES), F32)],
        scratch_shapes=[pltpu.VMEM((h, ML_DQK, ML_DV), F32), pltpu.VMEM((h, 1, ML_DQK), F32),
                        pltpu.VMEM((h, 1, LANES), F32)],
        compiler_params=_params("parallel", "arbitrary"),
    )(proj, proj, proj, proj, gates, bias, c0, n0, m0, g_h.reshape(1, vw))
    return outs


def _flat(x):
    return x.reshape(-1, x.shape[-1])


def _sb_layer(hp, hs, g, w_qkv, w_o, cache_k, cache_v):
    d = hp.shape[-1]
    w_q, w_kv = w_qkv[:, :d], w_qkv[:, d:]
    res = []
    for h, caches in ((hp, None), (hs, (cache_k, cache_v))):
        b, t, _ = h.shape
        (q,) = norm_matmul(_flat(h), g[0], w_q, [BF16])
        kv32, kv16 = norm_matmul_heads(h, g[0], w_kv, SB_HEADS)
        q = q.reshape(b, t, d)
        if caches is None:
            o = sb_attention_prompt(q, kv16)
        else:
            o = sb_attention_sample(q, kv16, caches[0], caches[1])
        h_new = matmul_norm_residual(_flat(o), w_o, g[1], _flat(h)).reshape(b, t, d)
        res.append((h_new, kv32[0], kv32[1]))
    return res


def _rw_layer(h, g, shift0, wkv0, p, w_o, ln_w, ln_b, r_k, chunk):
    b, t, d = h.shape
    r, lw, k, v, a, bb, gate, last = rwkv_project(h, shift0, g[0], p)
    y, s_fin = wkv_scan(r, lw, k, v, a, bb, gate, _pair_state(wkv0), ln_w, ln_b, r_k, chunk)
    h_new = matmul_norm_residual(_flat(y), w_o, g[1], _flat(h)).reshape(b, t, d)
    return h_new, last.reshape(b, d), _unpair_state(s_fin)


def _mla_layer(h, g, pos0, kv_cache, w):
    b, t, d = h.shape
    cos2, sin2 = _rope_tables(pos0 + jnp.arange(t))
    ckv, krope, kcat = mla_kv(h, g[0], w["dkv_ext"], w["g_kv"], cos2, sin2)
    (qd,) = norm_matmul(_flat(h), g[0], w["dq"], [BF16])
    (q_all,) = norm_matmul(qd, w["g_q"], w["uq_ext"], [BF16])
    q_full = mla_q_prepare(q_all.reshape(b, t, -1), w["uk_t"], cos2, sin2)
    if kv_cache is None:
        o = mla_attention(q_full, kcat, w["uv"], 0, True)
    else:
        c_ckv, c_kr = kv_cache
        kr16 = c_kr.astype(BF16)
        k_all = jnp.concatenate([jnp.concatenate([c_ckv.astype(BF16), kr16, kr16], axis=-1), kcat], axis=1)
        o = mla_attention(q_full, k_all, w["uv"], pos0, False)
    h_new = matmul_norm_residual(_flat(o), w["o"], g[1], _flat(h)).reshape(b, t, d)
    return h_new, ckv, krope


def _ml_layer(h, g, c0, n0, m0, chunk, w):
    b, t, d = h.shape
    (proj,) = norm_matmul(_flat(h), g[0], w["in"], [BF16])
    (gates,) = norm_matmul(_flat(h), g[0], w["if_pad"], [F32])
    hm, c, n, m = mlstm_scan(proj.reshape(b, t, -1), gates.reshape(b, t, LANES), w["b_pad"],
                             c0, n0.reshape(b, ML_HEADS, 1, ML_DQK),
                             jnp.broadcast_to(m0[:, :, None, None], (b, ML_HEADS, 1, LANES)), w["g_h"], chunk)
    h_new = matmul_norm_residual(_flat(hm), w["out"], g[1], _flat(h)).reshape(b, t, d)
    return h_new, c, n.reshape(b, ML_HEADS, ML_DQK), m[:, :, 0, 0]


def kernel(x_prompt, x_sample, mem_prompt, cache_sb_k, cache_sb_v, state_rwkv_shift, state_rwkv_wkv, cache_mla_ckv, cache_mla_krope, state_mlstm_c, state_mlstm_n, state_mlstm_m, cache_mem_k, cache_mem_v, norm_g, sb_w_qkv, sb_w_o, rw_mu, rw_w_r, rw_w_k, rw_w_v, rw_w_o, rw_w0, rw_w1, rw_w2, rw_a0, rw_a1, rw_a2, rw_g1, rw_g2, rw_k_k, rw_k_a, rw_r_k, rw_ln_w, rw_ln_b, mla_w_dq, mla_g_q, mla_w_uq, mla_w_dkv, mla_g_kv, mla_w_uk, mla_w_uv, mla_w_o, ml_w_in, ml_w_if, ml_b_if, ml_g_h, ml_w_out, w_mem_q, w_mem_kv, w_mem_o, w_ffn_up, w_ffn_down):
    bp, seq_p, d = x_prompt.shape
    bs, t_s, _ = x_sample.shape
    past = cache_sb_k.shape[1]
    depth = norm_g.shape[0]
    bf = lambda w: w.astype(BF16)
    pad_cols = lambda w, n: jnp.pad(w, ((0, 0), (0, n - w.shape[1])))
    pad_rows = lambda w, n: jnp.pad(w, ((0, n - w.shape[0]), (0, 0)))

    rw_p = dict(mu=rw_mu, w_r=bf(rw_w_r), w_k=bf(rw_w_k), w_v=bf(rw_w_v),
                w1=bf(pad_cols(rw_w1, RW_LORA_PAD)), w2=bf(pad_rows(rw_w2, RW_LORA_PAD)),
                a1=bf(pad_cols(rw_a1, RW_LORA_PAD)), a2=bf(pad_rows(rw_a2, RW_LORA_PAD)),
                g1=bf(rw_g1), g2=bf(rw_g2), w0=rw_w0, a0=rw_a0, k_k=rw_k_k, k_a=rw_k_a)

    kr_w = mla_w_dkv[:, MLA_KV_RANK:]
    uq = mla_w_uq.reshape(-1, MLA_HEADS, MLA_NOPE + MLA_ROPE)
    uq_rope = uq[:, :, MLA_NOPE:]
    rank_q = uq.shape[0]
    mla_w = dict(
        dkv_ext=bf(jnp.concatenate([mla_w_dkv[:, :MLA_KV_RANK], kr_w, kr_w, _rot_cols(kr_w), _rot_cols(kr_w)], axis=1)),
        g_kv=mla_g_kv, dq=bf(mla_w_dq), g_q=mla_g_q,
        uq_ext=bf(jnp.concatenate([uq[:, :, :MLA_NOPE].reshape(rank_q, -1), uq_rope.reshape(rank_q, -1),
                                   _rot_cols(uq_rope).reshape(rank_q, -1)], axis=1)),
        uk_t=bf(jnp.transpose(mla_w_uk, (1, 2, 0))),
        uv=bf(jnp.transpose(mla_w_uv, (1, 0, 2))),
        o=bf(mla_w_o))

    ml_w = {"in": bf(ml_w_in), "if_pad": bf(pad_cols(ml_w_if, LANES)),
            "b_pad": jnp.pad(ml_b_if, (0, LANES - ml_b_if.shape[0])).reshape(1, LANES),
            "g_h": ml_g_h, "out": bf(ml_w_out)}

    hp, hs = x_prompt, x_sample
    mem_k_list, mem_v_list = [], []
    outs = {}
    for layer in range(depth):
        g = norm_g[layer]
        kind = layer % 4
        if kind == 0:
            (hp, kp, vp), (hs, ks, vs) = _sb_layer(hp, hs, g, bf(sb_w_qkv), bf(sb_w_o), cache_sb_k, cache_sb_v)
            outs["sb"] = (kp, vp, ks, vs)
        elif kind == 1:
            n_h = d // RW_HD
            hp, sh_p, wkv_p = _rw_layer(hp, g, jnp.zeros((bp, d), F32), jnp.zeros((bp, n_h, RW_HD, RW_HD), F32),
                                        rw_p, bf(rw_w_o), rw_ln_w, rw_ln_b, rw_r_k, CHUNK)
            hs, sh_s, wkv_s = _rw_layer(hs, g, state_rwkv_shift, state_rwkv_wkv,
                                        rw_p, bf(rw_w_o), rw_ln_w, rw_ln_b, rw_r_k, CHUNK)
            outs["rw"] = (sh_p, wkv_p, sh_s, wkv_s)
        elif kind == 2:
            hp, ckv_p, kr_p = _mla_layer(hp, g, 0, None, mla_w)
            hs, ckv_s, kr_s = _mla_layer(hs, g, past, (cache_mla_ckv, cache_mla_krope), mla_w)
            outs["mla"] = (ckv_p, kr_p, ckv_s, kr_s)
        else:
            zc = jnp.zeros((bp, ML_HEADS, ML_DQK, ML_DV), F32)
            hp, c_p, n_p, m_p = _ml_layer(hp, g, zc, jnp.zeros((bp, ML_HEADS, ML_DQK), F32),
                                          jnp.zeros((bp, ML_HEADS), F32), CHUNK, ml_w)
            hs, c_s, n_s, m_s = _ml_layer(hs, g, state_mlstm_c, state_mlstm_n, state_mlstm_m, t_s, ml_w)
            outs["ml"] = (c_p, n_p, m_p, c_s, n_s, m_s)

        mkv, _ = norm_matmul_heads(mem_prompt, None, bf(w_mem_kv[layer]), MEM_HEADS)
        mk, mv = mkv[0], mkv[1]
        mem_k_list.append(mk)
        mem_v_list.append(mv)
        wq, wo = bf(w_mem_q[layer]), bf(w_mem_o[layer])
        hp = mem_attn_block(hp, g[2], wq, mkv, 0, mkv, 1, wo, g[3])
        hs = mem_attn_block(hs, g[2], wq, cache_mem_k, layer, cache_mem_v, layer, wo, g[3])

        wu, wd = bf(w_ffn_up[layer]), bf(w_ffn_down[layer])
        hp = ffn_block(_flat(hp), g[4], wu, wd, g[5]).reshape(bp, seq_p, d)
        hs = ffn_block(_flat(hs), g[4], wu, wd, g[5]).reshape(bs, t_s, d)

    return (hp, hs) + outs["sb"] + outs["rw"] + outs["mla"] + outs["ml"] + (
        jnp.stack(mem_k_list, axis=0), jnp.stack(mem_v_list, axis=0))
```

```python
import functools
import math

import jax
import jax.numpy as jnp
from jax import lax
from jax.experimental import pallas as pl
from jax.experimental.pallas import tpu as pltpu

F32, BF16 = jnp.float32, jnp.bfloat16

CHUNK = 64
NORM_EPS = 1e-6
SB_HEADS, SB_HD = 16, 128
RW_HD = 64
RW_LORA_PAD = 128
RW_GN_EPS = 64e-5
MLA_HEADS, MLA_KV_RANK, MLA_NOPE, MLA_ROPE, MLA_VD = 16, 512, 128, 64, 128
MLA_SCALE = (MLA_NOPE + MLA_ROPE) ** -0.5
MLA_KCAT = MLA_KV_RANK + 2 * MLA_ROPE
ROPE_THETA = 10000.0
ML_HEADS, ML_DQK, ML_DV = 8, 128, 256
MEM_HEADS, MEM_HD = 4, 128

V7X_VMEM_LIMIT_BYTES = 56 * 1024 * 1024
LANES = 128

NT_DIMS = (((1,), (1,)), ((), ()))


def _params(*sem):
    return pltpu.CompilerParams(dimension_semantics=sem, vmem_limit_bytes=V7X_VMEM_LIMIT_BYTES)


def _tile(n, pref, align=8):
    if n <= pref:
        return n
    for t in range(pref, 0, -1):
        if n % t == 0 and t % align == 0:
            return t
    return n


def _rms(x, g, eps=NORM_EPS):
    return x * lax.rsqrt(jnp.mean(x * x, axis=-1, keepdims=True) + eps) * g


def _dot(a, b):
    return jnp.dot(a, b, preferred_element_type=F32)


def _dot_nt(a, b):
    return lax.dot_general(a, b, NT_DIMS, preferred_element_type=F32)


def _split(x, parts):
    out = []
    for _ in range(parts):
        p = x.astype(BF16)
        out.append(p)
        x = x - p.astype(F32)
    return out


def _dot_split_lhs(x, b_bf16, parts=2):
    acc = None
    for p in _split(x, parts):
        t = _dot(p, b_bf16)
        acc = t if acc is None else acc + t
    return acc


def _dot_split_rhs(a_bf16, x, parts=2):
    acc = None
    for p in _split(x, parts):
        t = _dot(a_bf16, p)
        acc = t if acc is None else acc + t
    return acc


def _dot_hi(a, b):
    ah, al = _split(a, 2)
    bh, bl = _split(b, 2)
    return _dot(ah, bh) + (_dot(ah, bl) + _dot(al, bh))


def _softplus(z):
    return jnp.maximum(z, 0.0) + jnp.log1p(jnp.exp(-jnp.abs(z)))


def _iota(shape, dim):
    return lax.broadcasted_iota(jnp.int32, shape, dim)


def _pad_rows(x, rows):
    if x.shape[0] >= rows:
        return x
    return jnp.concatenate([x, jnp.zeros((rows - x.shape[0], x.shape[1]), x.dtype)], axis=0)


def _norm_mm_kernel(x_ref, g_ref, w_ref, *refs, normed, n_out):
    outs, u_ref = refs[:n_out], refs[n_out]

    @pl.when(pl.program_id(1) == 0)
    def _():
        x = x_ref[...].astype(F32)
        if normed:
            x = _rms(x, g_ref[...])
        u_ref[...] = x.astype(BF16)

    y = _dot(u_ref[...], w_ref[...])
    for o in outs:
        o[...] = y.astype(o.dtype)


def norm_matmul(x, g, w, out_dtypes, normed=True, tm=512, tn=1024):
    m, k = x.shape
    n = w.shape[1]
    tm, tn = _tile(m, tm), _tile(n, tn, LANES)
    if g is None:
        g = jnp.ones((k,), F32)
    outs = pl.pallas_call(
        functools.partial(_norm_mm_kernel, normed=normed, n_out=len(out_dtypes)), name="norm_mm",
        grid=(m // tm, n // tn),
        in_specs=[pl.BlockSpec((tm, k), lambda i, j: (i, 0)),
                  pl.BlockSpec((1, k), lambda i, j: (0, 0)),
                  pl.BlockSpec((k, tn), lambda i, j: (0, j))],
        out_specs=[pl.BlockSpec((tm, tn), lambda i, j: (i, j)) for _ in out_dtypes],
        out_shape=[jax.ShapeDtypeStruct((m, n), d) for d in out_dtypes],
        scratch_shapes=[pltpu.VMEM((tm, k), BF16)],
        compiler_params=_params("parallel", "arbitrary"),
    )(x, g.reshape(1, k), w)
    return outs


def _norm_mm_heads_kernel(x_ref, g_ref, w_ref, o4_ref, o16_ref, *, heads, hd, normed):
    x = x_ref[0]
    u = (_rms(x, g_ref[...]) if normed else x).astype(BF16)
    y = _dot(u, w_ref[...])
    o16_ref[0] = y.astype(BF16)
    for hh in range(heads):
        o4_ref[0, 0, :, hh, :] = y[:, hh * hd:(hh + 1) * hd]


def norm_matmul_heads(x, g, w, heads, tm=512):
    b, t, k = x.shape
    n = w.shape[1]
    hd = SB_HD
    wide = heads * hd
    s = n // wide
    tm = _tile(t, tm)
    normed = g is not None
    if not normed:
        g = jnp.ones((k,), F32)
    return pl.pallas_call(
        functools.partial(_norm_mm_heads_kernel, heads=heads, hd=hd, normed=normed), name="norm_mm_heads",
        grid=(b, t // tm, s),
        in_specs=[pl.BlockSpec((1, tm, k), lambda bi, i, j: (bi, i, 0)),
                  pl.BlockSpec((1, k), lambda bi, i, j: (0, 0)),
                  pl.BlockSpec((k, wide), lambda bi, i, j: (0, j))],
        out_specs=[pl.BlockSpec((1, 1, tm, heads, hd), lambda bi, i, j: (j, bi, i, 0, 0)),
                   pl.BlockSpec((1, tm, wide), lambda bi, i, j: (bi, i, j))],
        out_shape=[jax.ShapeDtypeStruct((s, b, t, heads, hd), F32),
                   jax.ShapeDtypeStruct((b, t, n), BF16)],
        compiler_params=_params("parallel", "parallel", "arbitrary"),
    )(x, g.reshape(1, k), w)


def _mm_norm_res_kernel(a_ref, w_ref, g_ref, h_ref, o_ref):
    y = _dot(a_ref[...], w_ref[...])
    o_ref[...] = h_ref[...] + _rms(y, g_ref[...])


def matmul_norm_residual(a, w, g, h, tm=512):
    m, k = a.shape
    d = w.shape[1]
    tm = _tile(m, tm)
    return pl.pallas_call(
        _mm_norm_res_kernel, name="mm_norm_res",
        grid=(m // tm,),
        in_specs=[pl.BlockSpec((tm, k), lambda i: (i, 0)),
                  pl.BlockSpec((k, d), lambda i: (0, 0)),
                  pl.BlockSpec((1, d), lambda i: (0, 0)),
                  pl.BlockSpec((tm, d), lambda i: (i, 0))],
        out_specs=pl.BlockSpec((tm, d), lambda i: (i, 0)),
        out_shape=jax.ShapeDtypeStruct((m, d), F32),
        compiler_params=_params("parallel"),
    )(a, w, g.reshape(1, d), h)


def _ffn_kernel(h_ref, g1_ref, wu_ref, wd_ref, g2_ref, o_ref, u_ref, acc_ref):
    f = pl.program_id(1)

    @pl.when(f == 0)
    def _():
        u_ref[...] = _rms(h_ref[...], g1_ref[...]).astype(BF16)
        acc_ref[...] = jnp.zeros_like(acc_ref)

    a = _dot(u_ref[...], wu_ref[0])
    a = jnp.square(jnp.maximum(a, 0.0)).astype(BF16)
    acc_ref[...] += _dot(a, wd_ref[0])

    @pl.when(f == pl.num_programs(1) - 1)
    def _():
        o_ref[...] = h_ref[...] + _rms(acc_ref[...], g2_ref[...])


def ffn_block(h, g_pre, w_up, w_down, layer, g_post, tm=512, tf=1024):
    m, d = h.shape
    ff = w_up.shape[2]
    tm, tf = _tile(m, tm), _tile(ff, tf, LANES)
    return pl.pallas_call(
        _ffn_kernel, name="ffn",
        grid=(m // tm, ff // tf),
        in_specs=[pl.BlockSpec((tm, d), lambda i, f: (i, 0)),
                  pl.BlockSpec((1, d), lambda i, f: (0, 0)),
                  pl.BlockSpec((1, d, tf), lambda i, f: (layer, 0, f)),
                  pl.BlockSpec((1, tf, d), lambda i, f: (layer, f, 0)),
                  pl.BlockSpec((1, d), lambda i, f: (0, 0))],
        out_specs=pl.BlockSpec((tm, d), lambda i, f: (i, 0)),
        out_shape=jax.ShapeDtypeStruct((m, d), F32),
        scratch_shapes=[pltpu.VMEM((tm, d), BF16), pltpu.VMEM((tm, d), F32)],
        compiler_params=_params("parallel", "arbitrary"),
    )(h, g_pre.reshape(1, d), w_up, w_down, g_post.reshape(1, d))


def _mem_attn_kernel(h_ref, g1_ref, wq_ref, mk_ref, mv_ref, wo_ref, g2_ref, o_ref):
    h = h_ref[0]
    u = _rms(h, g1_ref[...]).astype(BF16)
    q = (_dot(u, wq_ref[...]) * (MEM_HD ** -0.5)).astype(BF16)
    heads = []
    for hh in range(MEM_HEADS):
        sl = slice(hh * MEM_HD, (hh + 1) * MEM_HD)
        s = _dot_nt(q[:, sl], mk_ref[0, 0, :, hh, :].astype(BF16))
        p = jnp.exp(s - jnp.max(s, axis=-1, keepdims=True))
        o = _dot(p.astype(BF16), mv_ref[0, 0, :, hh, :].astype(BF16)) / jnp.sum(p, axis=-1, keepdims=True)
        heads.append(o.astype(BF16))
    y = _dot(jnp.concatenate(heads, axis=-1), wo_ref[...])
    o_ref[0] = h + _rms(y, g2_ref[...])


def mem_attn_block(h, g_pre, w_q, mk, k_at, mv, v_at, w_o, g_post, tm=512):
    b, t, d = h.shape
    mt, mw = mk.shape[2], mk.shape[3] * mk.shape[4]
    tm = _tile(t, tm)
    mem_spec = lambda at: pl.BlockSpec((1, 1, mt) + mk.shape[3:], lambda bi, i: (at, bi, 0, 0, 0))
    return pl.pallas_call(
        _mem_attn_kernel, name="mem_attn",
        grid=(b, t // tm),
        in_specs=[pl.BlockSpec((1, tm, d), lambda bi, i: (bi, i, 0)),
                  pl.BlockSpec((1, d), lambda bi, i: (0, 0)),
                  pl.BlockSpec((d, mw), lambda bi, i: (0, 0)),
                  mem_spec(k_at), mem_spec(v_at),
                  pl.BlockSpec((mw, d), lambda bi, i: (0, 0)),
                  pl.BlockSpec((1, d), lambda bi, i: (0, 0))],
        out_specs=pl.BlockSpec((1, tm, d), lambda bi, i: (bi, i, 0)),
        out_shape=jax.ShapeDtypeStruct((b, t, d), F32),
        compiler_params=_params("parallel", "parallel"),
    )(h, g_pre.reshape(1, d), w_q, mk, mv, w_o, g_post.reshape(1, d))


def _sb_keys_block(qs, ks, vs, carries, mask, upper):
    tq = qs[0].shape[0]
    zs = [_dot_nt(q, k) * (SB_HD ** -0.5) for q, k in zip(qs, ks)]
    sps = [_softplus(z) for z in zs]
    keeps = [-sp if mask is None else jnp.where(mask, -sp, 0.0) for sp in sps]
    incl = _dot_split_lhs(jnp.concatenate(keeps, axis=0), upper)
    incls = [incl[n * tq:(n + 1) * tq] for n in range(len(qs))]
    ws = [jnp.exp(z - sp + (inc - keep) + c)
          for z, sp, inc, keep, c in zip(zs, sps, incls, keeps, carries)]
    if mask is not None:
        ws = [jnp.where(mask, w, 0.0) for w in ws]
    pvs = [_dot(w.astype(BF16), v) for w, v in zip(ws, vs)]
    return pvs, [c + inc[:, :1] for c, inc in zip(carries, incls)]


def _upper_ones(tk):
    return (_iota((tk, tk), 0) >= _iota((tk, tk), 1)).astype(BF16)


SB_DEAD_LOG = -104.0


def _sb_alive(visited, total, c_ref):
    return jnp.logical_and(visited < total, jnp.max(c_ref[...]) > SB_DEAD_LOG)


def _sb_prompt_kernel(q_ref, k_ref, v_ref, o_ref, acc_ref, c_ref, *, tq, tk, heads):
    i = pl.program_id(2)
    lanes = [slice(n * SB_HD, (n + 1) * SB_HD) for n in range(heads)]
    qs = [q_ref[0, :, ls] for ls in lanes]
    upper = _upper_ones(tk)
    acc_ref[...] = jnp.zeros_like(acc_ref)
    c_ref[...] = jnp.zeros_like(c_ref)
    nd = tq // tk
    qpos = _iota((tq, tk), 0)
    kcol = _iota((tq, tk), 1)

    def visit(start, mask):
        ks = [k_ref[0, pl.ds(start, tk), ls] for ls in lanes]
        vs = [v_ref[0, pl.ds(start, tk), ls] for ls in lanes]
        pvs, cs = _sb_keys_block(qs, ks, vs, [c_ref[n] for n in range(heads)], mask, upper)
        for n, ls in enumerate(lanes):
            acc_ref[:, ls] += pvs[n]
            c_ref[n] = cs[n]

    for dd in reversed(range(nd)):
        visit(pl.multiple_of(i * tq + dd * tk, tk), (kcol + dd * tk) < qpos)

    def body(jj):
        visit(pl.multiple_of((i * nd - 1 - jj) * tk, tk), None)
        return jj + 1

    lax.while_loop(lambda jj: _sb_alive(jj, i * nd, c_ref), body, 0)
    o_ref[0] = acc_ref[...].astype(o_ref.dtype)


def sb_attention_prompt(q, kv, tq=128, tk=128, heads=8):
    b, t, hd = q.shape
    tq, tk = min(tq, t), min(tk, t)
    groups = hd // (heads * SB_HD)
    wide = heads * SB_HD
    return pl.pallas_call(
        functools.partial(_sb_prompt_kernel, tq=tq, tk=tk, heads=heads), name="sb_prompt",
        grid=(b, groups, t // tq),
        in_specs=[pl.BlockSpec((1, tq, wide), lambda bi, hi, i: (bi, i, hi)),
                  pl.BlockSpec((1, t, wide), lambda bi, hi, i: (bi, 0, hi)),
                  pl.BlockSpec((1, t, wide), lambda bi, hi, i: (bi, 0, groups + hi))],
        out_specs=pl.BlockSpec((1, tq, wide), lambda bi, hi, i: (bi, i, hi)),
        out_shape=jax.ShapeDtypeStruct((b, t, hd), BF16),
        scratch_shapes=[pltpu.VMEM((tq, wide), F32), pltpu.VMEM((heads, tq, 1), F32)],
        compiler_params=_params("parallel", "parallel", "arbitrary"),
    )(q, kv, kv)


def _sb_sample_kernel(q_ref, kn_ref, vn_ref, kc_ref, vc_ref, o_ref, acc_ref, c_ref, *, ts, tk, heads):
    j = pl.program_id(1)
    lanes = [slice(n * SB_HD, (n + 1) * SB_HD) for n in range(heads)]
    qs = [q_ref[0, :, ls] for ls in lanes]

    @pl.when(j == 0)
    def _():
        mask = _iota((ts, ts), 1) < _iota((ts, ts), 0)
        pvs, cs = _sb_keys_block(qs, [kn_ref[0, :, ls] for ls in lanes], [vn_ref[0, :, ls] for ls in lanes],
                                 [jnp.zeros((ts, 1), F32)] * heads, mask, _upper_ones(ts))
        for n, ls in enumerate(lanes):
            acc_ref[:, ls] = pvs[n]
            c_ref[n] = cs[n]

    @pl.when(jnp.max(c_ref[...]) > SB_DEAD_LOG)
    def _():
        ks = [kc_ref[0, :, n, :].astype(BF16) for n in range(heads)]
        vs = [vc_ref[0, :, n, :].astype(BF16) for n in range(heads)]
        pvs, cs = _sb_keys_block(qs, ks, vs, [c_ref[n] for n in range(heads)], None, _upper_ones(tk))
        for n, ls in enumerate(lanes):
            acc_ref[:, ls] += pvs[n]
            c_ref[n] = cs[n]

    @pl.when(j == pl.num_programs(1) - 1)
    def _():
        o_ref[0] = acc_ref[...].astype(o_ref.dtype)


def sb_attention_sample(q, kv_new, cache_k, cache_v, tk=256):
    b, ts, hd = q.shape
    past, h = cache_k.shape[1], cache_k.shape[2]
    tk = _tile(past, tk)
    nkb = past // tk
    cache_spec = pl.BlockSpec((1, tk, h, SB_HD), lambda bi, j: (bi, nkb - 1 - j, 0, 0))
    return pl.pallas_call(
        functools.partial(_sb_sample_kernel, ts=ts, tk=tk, heads=h), name="sb_sample",
        grid=(b, nkb),
        in_specs=[pl.BlockSpec((1, ts, hd), lambda bi, j: (bi, 0, 0)),
                  pl.BlockSpec((1, ts, hd), lambda bi, j: (bi, 0, 0)),
                  pl.BlockSpec((1, ts, hd), lambda bi, j: (bi, 0, 1)),
                  cache_spec, cache_spec],
        out_specs=pl.BlockSpec((1, ts, hd), lambda bi, j: (bi, 0, 0)),
        out_shape=jax.ShapeDtypeStruct((b, ts, hd), BF16),
        scratch_shapes=[pltpu.VMEM((ts, hd), F32), pltpu.VMEM((h, ts, 1), F32)],
        compiler_params=_params("parallel", "arbitrary"),
    )(q, kv_new, kv_new, cache_k, cache_v)


def _rw_proj_kernel(h_ref, halo_ref, shift_ref, g_ref, mu_ref, wr_ref, wk_ref, wv_ref,
                    w1_ref, w2_ref, a1_ref, a2_ref, g1_ref, g2_ref,
                    w0_ref, a0_ref, kk_ref, ka_ref, seg_ref,
                    r_out, lw_out, k_out, v_out, a_out, b_out, g_out, last_out,
                    xr_ref, xk_ref, xv_ref, hw_ref, ha_ref, hg_ref, *, tm):
    i, j = pl.program_id(1), pl.program_id(2)

    @pl.when(j == 0)
    def _():
        u = _rms(h_ref[0], g_ref[...])
        before = _rms(halo_ref[0], g_ref[...])[7:8]
        first = jnp.where(i == 0, shift_ref[0], before)
        prev = jnp.where(_iota((tm, 1), 0) == 0, first, pltpu.roll(u, 1, 0))
        xx = prev - u
        mu = mu_ref[...]
        xr_ref[...] = (u + xx * mu[0:1]).astype(BF16)
        xw = (u + xx * mu[1:2]).astype(BF16)
        xk_ref[...] = (u + xx * mu[2:3]).astype(BF16)
        xv_ref[...] = (u + xx * mu[3:4]).astype(BF16)
        xa = (u + xx * mu[4:5]).astype(BF16)
        xg = (u + xx * mu[5:6]).astype(BF16)
        hw_ref[...] = jnp.tanh(_dot(xw, w1_ref[...])).astype(BF16)
        ha_ref[...] = _dot(xa, a1_ref[...]).astype(BF16)
        hg_ref[...] = jax.nn.sigmoid(_dot(xg, g1_ref[...])).astype(BF16)

        @pl.when(i == pl.num_programs(1) - 1)
        def _():
            last_out[0] = u[tm - 1:tm]

    r = _dot(xr_ref[...], wr_ref[...])
    k = _dot(xk_ref[...], wk_ref[...])
    v = _dot(xv_ref[...], wv_ref[...])
    w_log = -_softplus(-(w0_ref[...] + _dot(hw_ref[...], w2_ref[...]))) - 0.5
    a = jax.nn.sigmoid(a0_ref[...] + _dot(ha_ref[...], a2_ref[...]))
    gate = _dot(hg_ref[...], g2_ref[...])
    kk = k * kk_ref[...]
    ss = _dot_split_lhs(kk * kk, seg_ref[...])
    kk = kk * lax.rsqrt(jnp.maximum(ss, 1e-24))
    r_out[0] = r.astype(BF16)
    lw_out[0] = -jnp.exp(w_log)
    k_out[0] = (k * (1.0 + (a - 1.0) * ka_ref[...])).astype(BF16)
    v_out[0] = v.astype(BF16)
    a_out[0] = (-kk).astype(BF16)
    b_out[0] = (kk * a).astype(BF16)
    g_out[0] = gate.astype(BF16)


def _segment_ones(n, seg):
    idx = jnp.arange(n) // seg
    return (idx[:, None] == idx[None, :]).astype(BF16)


def rwkv_project(h, shift0, g, p, tm=512, tn=256):
    b, t, d = h.shape
    tm, tn = _tile(t, tm), _tile(d, tn, LANES)
    lp, gl = RW_LORA_PAD, p["g1"].shape[1]
    tok = lambda bi, i, j: (bi, i, j)
    col = lambda bi, i, j: (0, j)
    fix = lambda bi, i, j: (0, 0)
    tok_spec = pl.BlockSpec((1, tm, tn), tok)
    outs = pl.pallas_call(
        functools.partial(_rw_proj_kernel, tm=tm), name="rw_proj",
        grid=(b, t // tm, d // tn),
        in_specs=[pl.BlockSpec((1, tm, d), lambda bi, i, j: (bi, i, 0)),
                  pl.BlockSpec((1, 8, d), lambda bi, i, j: (bi, jnp.maximum(i * (tm // 8) - 1, 0), 0)),
                  pl.BlockSpec((1, 1, d), lambda bi, i, j: (bi, 0, 0)),
                  pl.BlockSpec((1, d), fix),
                  pl.BlockSpec((6, d), fix),
                  pl.BlockSpec((d, tn), col), pl.BlockSpec((d, tn), col), pl.BlockSpec((d, tn), col),
                  pl.BlockSpec((d, lp), fix), pl.BlockSpec((lp, tn), col),
                  pl.BlockSpec((d, lp), fix), pl.BlockSpec((lp, tn), col),
                  pl.BlockSpec((d, gl), fix), pl.BlockSpec((gl, tn), col),
                  pl.BlockSpec((1, tn), col), pl.BlockSpec((1, tn), col),
                  pl.BlockSpec((1, tn), col), pl.BlockSpec((1, tn), col),
                  pl.BlockSpec((tn, tn), fix)],
        out_specs=[tok_spec] * 7 + [pl.BlockSpec((1, 1, d), lambda bi, i, j: (bi, 0, 0))],
        out_shape=[jax.ShapeDtypeStruct((b, t, d), BF16), jax.ShapeDtypeStruct((b, t, d), F32)]
        + [jax.ShapeDtypeStruct((b, t, d), BF16)] * 5 + [jax.ShapeDtypeStruct((b, 1, d), F32)],
        scratch_shapes=[pltpu.VMEM((tm, d), BF16)] * 3
        + [pltpu.VMEM((tm, lp), BF16), pltpu.VMEM((tm, lp), BF16), pltpu.VMEM((tm, gl), BF16)],
        compiler_params=_params("parallel", "arbitrary", "arbitrary"),
    )(h, h, shift0.reshape(b, 1, d), g.reshape(1, d), p["mu"], p["w_r"], p["w_k"], p["w_v"],
      p["w1"], p["w2"], p["a1"], p["a2"], p["g1"], p["g2"],
      p["w0"].reshape(1, d), p["a0"].reshape(1, d), p["k_k"].reshape(1, d), p["k_a"].reshape(1, d),
      _segment_ones(tn, RW_HD))
    return outs


def _wkv_kernel(r_ref, lw_ref, k_ref, v_ref, a_ref, b_ref, g_ref, s0_ref, lnw_ref, lnb_ref, rk_ref,
                seg_ref, y_out, s_out, s_ref, *, L, pairs, solve_parts):
    c = pl.program_id(2)

    @pl.when(c == 0)
    def _():
        s_ref[...] = s0_ref[0]

    L2 = 2 * L
    rr, cc = _iota((L2, L2), 0), _iota((L2, L2), 1)
    same_head = (rr // L) == (cc // L)
    strict = same_head & ((rr % L) > (cc % L))
    incl = same_head & ((rr % L) >= (cc % L))
    eye = (rr == cc).astype(F32)
    tri = (_iota((L, L), 0) >= _iota((L, L), 1)).astype(BF16)
    head0 = _iota((1, LANES), 1) < RW_HD
    blockdiag = (_iota((LANES, LANES), 0) // RW_HD) == (_iota((LANES, LANES), 1) // RW_HD)
    seg = seg_ref[...]
    rows2 = max(L2, LANES)
    steps = int(math.log2(L)) - 1
    P = range(pairs)
    lanes = [slice(pp * LANES, (pp + 1) * LANES) for pp in P]

    def stack(x):
        return jnp.concatenate([jnp.where(head0, x, 0.0), jnp.where(head0, 0.0, x)], axis=0)

    def unstack(x2):
        return x2[:L] + x2[L:]

    lw = [lw_ref[0, :, ls] for ls in lanes]
    cum = [_dot_split_rhs(tri, x, 3) for x in lw]
    p_in = [jnp.exp(x) for x in cum]
    p_ex = [jnp.exp(x - y) for x, y in zip(cum, lw)]
    p_inv = [jnp.exp(-x) for x in cum]
    p_end = [x[L - 1:L] for x in p_in]
    r = [r_ref[0, :, ls].astype(F32) for ls in lanes]
    k = [k_ref[0, :, ls].astype(F32) for ls in lanes]
    v = [v_ref[0, :, ls].astype(F32) for ls in lanes]
    at = [a_ref[0, :, ls].astype(F32) * x for ls, x in zip(lanes, p_ex)]
    bt = [b_ref[0, :, ls].astype(F32) * x for ls, x in zip(lanes, p_inv)]
    kt = [x * y for x, y in zip(k, p_inv)]
    rt = [x * y for x, y in zip(r, p_in)]
    lhs = [jnp.concatenate([stack(x), stack(y)], axis=0).astype(BF16) for x, y in zip(at, rt)]
    rhs = [jnp.concatenate([stack(x), stack(y)], axis=0).astype(BF16) for x, y in zip(bt, kt)]
    gram = [_dot_nt(x, y) for x, y in zip(lhs, rhs)]
    s_old = [s_ref[pp] for pp in P]
    on_s = [_dot_nt(x, y.astype(BF16)) for x, y in zip(lhs, s_old)]
    v2 = [stack(x).astype(BF16) for x in v]
    n_ab = [jnp.where(strict, x[:L2, :L2], 0.0) for x in gram]
    n_ak = [jnp.where(strict, x[:L2, L2:], 0.0).astype(BF16) for x in gram]
    w2 = [x[:L2] + _dot(y, z) for x, y, z in zip(on_s, n_ak, v2)]
    t_acc, pw = [eye + x for x in n_ab], n_ab
    for _ in range(steps):
        pw = [_solve_dot(x, x, solve_parts) for x in pw]
        t_acc = [x + _solve_dot(x, y, solve_parts) for x, y in zip(t_acc, pw)]
    u2 = [_solve_dot(x, y, solve_parts) for x, y in zip(t_acc, w2)]
    m_rb = [jnp.where(incl, x[L2:, :L2], 0.0).astype(BF16) for x in gram]
    m_rk = [jnp.where(incl, x[L2:, L2:], 0.0).astype(BF16) for x in gram]
    y2 = [x[L2:] + _dot(a_, b_.astype(BF16)) + _dot(c_, d_)
          for x, a_, b_, c_, d_ in zip(on_s, m_rb, u2, m_rk, v2)]
    u = [unstack(x) for x in u2]
    y = [unstack(x) for x in y2]
    x_keys = [_pad_rows(jnp.concatenate([b_ * e_, k_ * e_], axis=0), rows2).astype(BF16)
              for b_, k_, e_ in zip(bt, kt, p_end)]
    z_vals = [_pad_rows(jnp.concatenate([u_, v_], axis=0), rows2) for u_, v_ in zip(u, v)]
    upd = [_dot(z_.T.astype(BF16), x_) for z_, x_ in zip(z_vals, x_keys)]
    for pp in P:
        s_new = s_old[pp] * p_end[pp] + jnp.where(blockdiag, upd[pp], 0.0)
        s_ref[pp] = s_new
        s_out[0, pp] = s_new

    mean = [_dot_split_lhs(x, seg) * (1.0 / RW_HD) for x in y]
    dev = [x - m_ for x, m_ in zip(y, mean)]
    var = [_dot_split_lhs(x * x, seg) * (1.0 / RW_HD) for x in dev]
    bonus = [_dot_split_lhs(r_ * k_ * rk_ref[:, ls], seg) * v_ for r_, k_, v_, ls in zip(r, k, v, lanes)]
    for pp, ls in enumerate(lanes):
        yn = dev[pp] * lax.rsqrt(var[pp] + RW_GN_EPS) * lnw_ref[:, ls] + lnb_ref[:, ls]
        y_out[0, :, ls] = ((yn + bonus[pp]) * g_ref[0, :, ls].astype(F32)).astype(BF16)


def _solve_dot(a, b, parts):
    if parts == 1:
        return _dot(a.astype(BF16), b.astype(BF16))
    return _dot_hi(a, b)


def wkv_scan(r, lw, k, v, a, b, gate, s0, ln_w, ln_b, r_k, chunk, pairs=16, solve_parts=1):
    bsz, t, d = r.shape
    L = min(chunk, t)
    npairs = d // LANES
    pairs = min(pairs, npairs)
    w = pairs * LANES
    tok = pl.BlockSpec((1, L, w), lambda bi, pi, c: (bi, c, pi))
    vec = pl.BlockSpec((1, w), lambda bi, pi, c: (0, pi))
    st = pl.BlockSpec((1, pairs, LANES, LANES), lambda bi, pi, c: (bi, pi, 0, 0))
    y, s_fin = pl.pallas_call(
        functools.partial(_wkv_kernel, L=L, pairs=pairs, solve_parts=solve_parts), name="wkv",
        grid=(bsz, npairs // pairs, t // L),
        in_specs=[tok] * 7 + [st, vec, vec, vec, pl.BlockSpec((LANES, LANES), lambda bi, pi, c: (0, 0))],
        out_specs=[tok, st],
        out_shape=[jax.ShapeDtypeStruct((bsz, t, d), BF16),
                   jax.ShapeDtypeStruct((bsz, npairs, LANES, LANES), F32)],
        scratch_shapes=[pltpu.VMEM((pairs, LANES, LANES), F32)],
        compiler_params=_params("parallel", "parallel", "arbitrary"),
    )(r, lw, k, v, a, b, gate, s0, ln_w.reshape(1, d), ln_b.reshape(1, d), r_k.reshape(1, d),
      _segment_ones(LANES, RW_HD))
    return y, s_fin


def _pair_state(s):
    b, h, n, _ = s.shape
    s = s.reshape(b, h // 2, 2, n, n)
    z = jnp.zeros_like(s[:, :, 0])
    top = jnp.concatenate([s[:, :, 0], z], axis=-1)
    bot = jnp.concatenate([z, s[:, :, 1]], axis=-1)
    return jnp.concatenate([top, bot], axis=-2)


def _unpair_state(sp):
    n = RW_HD
    b, hp = sp.shape[:2]
    return jnp.stack([sp[:, :, :n, :n], sp[:, :, n:, n:]], axis=2).reshape(b, 2 * hp, n, n)


def _mla_kv_kernel(h_ref, g_ref, w_ref, gkv_ref, cos_ref, sin_ref, ckv_out, kr_out, kcat_out):
    u = _rms(h_ref[0], g_ref[...]).astype(BF16)
    kv = _dot(u, w_ref[...])
    r = MLA_KV_RANK
    c = _rms(kv[:, :r], gkv_ref[...])
    kr2 = kv[:, r:r + LANES] * cos_ref[...] + kv[:, r + LANES:r + 2 * LANES] * sin_ref[...]
    ckv_out[0] = c
    kr_out[0] = kr2[:, :MLA_ROPE]
    kcat_out[0] = jnp.concatenate([c, kr2], axis=-1).astype(BF16)


def mla_kv(h, g, w_ext, g_kv, cos2, sin2, tm=512):
    b, t, d = h.shape
    tm = _tile(t, tm)
    n = w_ext.shape[1]
    return pl.pallas_call(
        _mla_kv_kernel, name="mla_kv",
        grid=(b, t // tm),
        in_specs=[pl.BlockSpec((1, tm, d), lambda bi, i: (bi, i, 0)),
                  pl.BlockSpec((1, d), lambda bi, i: (0, 0)),
                  pl.BlockSpec((d, n), lambda bi, i: (0, 0)),
                  pl.BlockSpec((1, MLA_KV_RANK), lambda bi, i: (0, 0)),
                  pl.BlockSpec((tm, LANES), lambda bi, i: (i, 0)),
                  pl.BlockSpec((tm, LANES), lambda bi, i: (i, 0))],
        out_specs=[pl.BlockSpec((1, tm, MLA_KV_RANK), lambda bi, i: (bi, i, 0)),
                   pl.BlockSpec((1, tm, MLA_ROPE), lambda bi, i: (bi, i, 0)),
                   pl.BlockSpec((1, tm, MLA_KCAT), lambda bi, i: (bi, i, 0))],
        out_shape=[jax.ShapeDtypeStruct((b, t, MLA_KV_RANK), F32),
                   jax.ShapeDtypeStruct((b, t, MLA_ROPE), F32),
                   jax.ShapeDtypeStruct((b, t, MLA_KCAT), BF16)],
        compiler_params=_params("parallel", "parallel"),
    )(h, g.reshape(1, d), w_ext, g_kv.reshape(1, MLA_KV_RANK), cos2, sin2)


def _mla_q_kernel(nope_ref, rope_ref, rot_ref, cos_ref, sin_ref, wuk_ref, o_ref):
    hh = pl.program_id(2)
    lat = _dot(nope_ref[0], wuk_ref[0]) * MLA_SCALE
    rope = (rope_ref[0].astype(F32) * cos_ref[...] + rot_ref[0].astype(F32) * sin_ref[...]) * MLA_SCALE
    mine = (_iota((1, LANES), 1) // MLA_ROPE) == (hh % 2)
    rope = jnp.where(mine, rope, 0.0)
    o_ref[0, 0] = jnp.concatenate([lat, rope], axis=-1).astype(BF16)


def mla_q_prepare(q_all, w_uk_t, cos2, sin2, tq=512):
    b, t, _ = q_all.shape
    tq = _tile(t, tq)
    h = MLA_HEADS
    nb = h * MLA_NOPE // LANES
    rb = h * MLA_ROPE // LANES
    return pl.pallas_call(
        _mla_q_kernel, name="mla_q",
        grid=(b, t // tq, h),
        in_specs=[pl.BlockSpec((1, tq, LANES), lambda bi, i, hh: (bi, i, hh)),
                  pl.BlockSpec((1, tq, LANES), lambda bi, i, hh: (bi, i, nb + hh // 2)),
                  pl.BlockSpec((1, tq, LANES), lambda bi, i, hh: (bi, i, nb + rb + hh // 2)),
                  pl.BlockSpec((tq, LANES), lambda bi, i, hh: (i, 0)),
                  pl.BlockSpec((tq, LANES), lambda bi, i, hh: (i, 0)),
                  pl.BlockSpec((1, MLA_NOPE, MLA_KV_RANK), lambda bi, i, hh: (hh, 0, 0))],
        out_specs=pl.BlockSpec((1, 1, tq, MLA_KCAT), lambda bi, i, hh: (bi, hh, i, 0)),
        out_shape=jax.ShapeDtypeStruct((b, h, t, MLA_KCAT), BF16),
        compiler_params=_params("parallel", "parallel", "arbitrary"),
    )(q_all, q_all, q_all, cos2, sin2, w_uk_t)


def _mla_attn_kernel(q_ref, k_ref, wuv_ref, o_ref, acc_ref, m_ref, l_ref, *, tq, tk, q0, tail, causal_blocks,
                     groups):
    i = pl.program_id(1)
    h = MLA_HEADS
    hg = h // groups
    rows = hg * tq
    qs = [q_ref[0, g * hg:(g + 1) * hg].reshape(rows, MLA_KCAT) for g in range(groups)]
    acc_ref[...] = jnp.zeros_like(acc_ref)
    m_ref[...] = jnp.full_like(m_ref, -jnp.inf)
    l_ref[...] = jnp.zeros_like(l_ref)
    qchunk = (q0 + i * tq + _iota((rows, 1), 0) % tq) // CHUNK

    def visit(k, start, masked):
        n = k.shape[0]
        kv = k[:, :MLA_KV_RANK]
        visible = ((start + _iota((1, n), 1)) // CHUNK <= qchunk) if masked else None
        s_next = _dot_nt(qs[0], k)
        for g in range(groups):
            s = s_next
            if g + 1 < groups:
                s_next = _dot_nt(qs[g + 1], k)
            if masked:
                s = jnp.where(visible, s, -jnp.inf)
            rs = pl.ds(g * rows, rows)
            m_old = m_ref[rs]
            m_new = jnp.maximum(m_old, jnp.max(s, axis=-1, keepdims=True))
            alpha = jnp.exp(m_old - m_new)
            p = jnp.exp(s - m_new)
            l_ref[rs] = alpha * l_ref[rs] + jnp.sum(p, axis=-1, keepdims=True)
            acc_ref[rs] = alpha * acc_ref[rs] + _dot(p.astype(BF16), kv)
            m_ref[rs] = m_new

    def body(masked, j, carry):
        start = pl.multiple_of(j * tk, tk)
        visit(k_ref[0, pl.ds(start, tk), :], start, masked)
        return carry

    klen = k_ref.shape[1]
    nblk = (klen - tail) // tk
    if causal_blocks:
        nblk = jnp.minimum(((i + 1) * tq + tk - 1) // tk, nblk)
    first_chunk = (q0 + i * tq) // CHUNK
    n_full = jnp.minimum(((first_chunk + 1) * CHUNK) // tk, nblk)
    lax.fori_loop(0, n_full, functools.partial(body, False), 0)
    lax.fori_loop(n_full, nblk, functools.partial(body, True), 0)
    if tail:
        start = klen - tail
        visit(k_ref[0, start:, :], start, True)
    o_lat = (acc_ref[...] / l_ref[...]).astype(BF16)
    outs = [_dot(o_lat[hh * tq:(hh + 1) * tq], wuv_ref[hh]).astype(BF16) for hh in range(h)]
    o_ref[0] = jnp.concatenate(outs, axis=-1)


def mla_attention(q_full, kcat, w_uv, q0, causal_blocks, tq=128, tk=512):
    b, h, t, _ = q_full.shape
    klen = kcat.shape[1]
    tq = _tile(t, tq, 16)
    tk = min(tk, klen)
    tail = klen % tk
    if causal_blocks:
        assert tail == 0 and tk % CHUNK == 0 and q0 == 0 and klen == t
    return pl.pallas_call(
        functools.partial(_mla_attn_kernel, tq=tq, tk=tk, q0=q0, tail=tail, causal_blocks=causal_blocks,
                          groups=4),
        name="mla_attn",
        grid=(b, t // tq),
        in_specs=[pl.BlockSpec((1, h, tq, MLA_KCAT), lambda bi, i: (bi, 0, i, 0)),
                  pl.BlockSpec((1, klen, MLA_KCAT), lambda bi, i: (bi, 0, 0)),
                  pl.BlockSpec((h, MLA_KV_RANK, MLA_VD), lambda bi, i: (0, 0, 0))],
        out_specs=pl.BlockSpec((1, tq, h * MLA_VD), lambda bi, i: (bi, i, 0)),
        out_shape=jax.ShapeDtypeStruct((b, t, h * MLA_VD), BF16),
        scratch_shapes=[pltpu.VMEM((h * tq, MLA_KV_RANK), F32), pltpu.VMEM((h * tq, 1), F32),
                        pltpu.VMEM((h * tq, 1), F32)],
        compiler_params=_params("parallel", "arbitrary"),
    )(q_full, kcat, w_uv)


def _rope_tables(pos):
    half = MLA_ROPE // 2
    inv_freq = ROPE_THETA ** (-jnp.arange(half, dtype=F32) / half)
    ang = pos.astype(F32)[:, None] * inv_freq[None, :]
    cos, sin = jnp.cos(ang), jnp.sin(ang)
    return jnp.tile(cos, (1, 4)), jnp.tile(sin, (1, 4))


def _rot_cols(w):
    half = MLA_ROPE // 2
    return jnp.concatenate([-w[..., half:], w[..., :half]], axis=-1)


def _mlstm_kernel(q_ref, k_ref, v_ref, og_ref, gate_ref, bias_ref, c0_ref, n0_ref, m0_ref, gh_ref,
                  h_out, c_out, n_out, m_out, c_ref, n_ref, m_ref, *, L, heads):
    ci = pl.program_id(1)

    @pl.when(ci == 0)
    def _():
        c_ref[...] = c0_ref[0]
        n_ref[...] = n0_ref[0]
        m_ref[...] = m0_ref[0]

    tt, ss = _iota((L, L), 0), _iota((L, L), 1)
    causal = tt >= ss
    tri = causal.astype(BF16)
    scale = ML_DQK ** -0.5
    rows = max(L, LANES)

    gates = gate_ref[0] + bias_ref[...]
    log_f = -_softplus(-gates)
    cum_all = _dot_split_rhs(tri, log_f, 3)
    gates_t = _pad_rows(gates, rows).T
    cum_t = _pad_rows(cum_all, rows).T

    H = range(heads)
    qk_l = [slice(hh * ML_DQK, (hh + 1) * ML_DQK) for hh in H]
    v_l = [slice(hh * ML_DV, (hh + 1) * ML_DV) for hh in H]
    q = [q_ref[0, :, ls] for ls in qk_l]
    k = [k_ref[0, :, ls] for ls in qk_l]
    v = [v_ref[0, :, ls] for ls in v_l]
    i_c = [gates[:, hh:hh + 1] for hh in H]
    cum_c = [cum_all[:, heads + hh:heads + hh + 1] for hh in H]
    i_r = [gates_t[hh:hh + 1, :L] for hh in H]
    cum_r = [cum_t[heads + hh:heads + hh + 1, :L] for hh in H]
    m_old = [m_ref[hh][:, :1] for hh in H]
    c_old = [c_ref[hh] for hh in H]
    n_old = [n_ref[hh] for hh in H]

    qk = [_dot_nt(a_, b_) for a_, b_ in zip(q, k)]
    q_c = [_dot(a_, b_.astype(BF16)) for a_, b_ in zip(q, c_old)]
    q_n = [jnp.sum(a_.astype(F32) * b_, axis=1, keepdims=True) for a_, b_ in zip(q, n_old)]
    dmat = [jnp.where(causal, a_ - b_ + c_, -jnp.inf) for a_, b_, c_ in zip(cum_c, cum_r, i_r)]
    inter = [a_ + b_ for a_, b_ in zip(cum_c, m_old)]
    m_t = [jnp.maximum(jnp.max(a_, axis=1, keepdims=True), b_) for a_, b_ in zip(dmat, inter)]
    s = [a_ * scale * jnp.exp(b_ - c_) for a_, b_, c_ in zip(qk, dmat, m_t)]
    inter_w = [jnp.exp(a_ - b_) * scale for a_, b_ in zip(inter, m_t)]
    num = [_dot(a_.astype(BF16), b_) + c_ * d_ for a_, b_, c_, d_ in zip(s, v, inter_w, q_c)]
    den = [jnp.sum(a_, axis=1, keepdims=True) + b_ * c_ for a_, b_, c_ in zip(s, inter_w, q_n)]
    out = [a_ / jnp.maximum(jnp.abs(b_), jnp.exp(-c_)) for a_, b_, c_ in zip(num, den, m_t)]

    last = [a_[L - 1:L] for a_ in cum_c]
    src = [a_ - b_ + c_ for a_, b_, c_ in zip(last, cum_c, i_c)]
    m_new = [jnp.maximum(a_ + b_, jnp.max(c_, axis=0, keepdims=True)) for a_, b_, c_ in zip(last, m_old, src)]
    keep = [jnp.exp(a_ + b_ - c_) for a_, b_, c_ in zip(last, m_old, m_new)]
    kw = [a_.astype(F32) * jnp.exp(b_ - c_) for a_, b_, c_ in zip(k, src, m_new)]
    upd = [_dot(_pad_rows(a_, rows).T.astype(BF16), _pad_rows(b_, rows)) for a_, b_ in zip(kw, v)]
    for hh in H:
        c_new = keep[hh] * c_old[hh] + upd[hh]
        n_new = keep[hh] * n_old[hh] + jnp.sum(kw[hh], axis=0, keepdims=True)
        c_ref[hh], n_ref[hh] = c_new, n_new
        m_ref[hh] = jnp.broadcast_to(m_new[hh], (1, LANES))
        c_out[0, hh], n_out[0, hh] = c_new, n_new
        m_out[0, hh] = jnp.broadcast_to(m_new[hh], (1, LANES))
    for hh in H:
        hn = _rms(out[hh], gh_ref[:, v_l[hh]])
        gate = jax.nn.sigmoid(og_ref[0, :, v_l[hh]].astype(F32))
        h_out[0, :, v_l[hh]] = (hn * gate).astype(BF16)


def mlstm_scan(proj, gates, bias, c0, n0, m0, g_h, chunk):
    b, t, _ = proj.shape
    h = ML_HEADS
    L = min(chunk, t)
    qw, vw = h * ML_DQK, h * ML_DV
    st = lambda shape: pl.BlockSpec((1,) + shape, lambda bi, c: (bi, 0, 0, 0))
    outs = pl.pallas_call(
        functools.partial(_mlstm_kernel, L=L, heads=h), name="mlstm",
        grid=(b, t // L),
        in_specs=[pl.BlockSpec((1, L, qw), lambda bi, c: (bi, c, 0)),
                  pl.BlockSpec((1, L, qw), lambda bi, c: (bi, c, 1)),
                  pl.BlockSpec((1, L, vw), lambda bi, c: (bi, c, 2 * qw // vw)),
                  pl.BlockSpec((1, L, vw), lambda bi, c: (bi, c, 2 * qw // vw + 1)),
                  pl.BlockSpec((1, L, LANES), lambda bi, c: (bi, c, 0)),
                  pl.BlockSpec((1, LANES), lambda bi, c: (0, 0)),
                  st((h, ML_DQK, ML_DV)), st((h, 1, ML_DQK)), st((h, 1, LANES)),
                  pl.BlockSpec((1, vw), lambda bi, c: (0, 0))],
        out_specs=[pl.BlockSpec((1, L, vw), lambda bi, c: (bi, c, 0)),
                   st((h, ML_DQK, ML_DV)), st((h, 1, ML_DQK)), st((h, 1, LANES))],
        out_shape=[jax.ShapeDtypeStruct((b, t, vw), BF16),
                   jax.ShapeDtypeStruct((b, h, ML_DQK, ML_DV), F32),
                   jax.ShapeDtypeStruct((b, h, 1, ML_DQK), F32),
                   jax.ShapeDtypeStruct((b, h, 1, LANES), F32)],
        scratch_shapes=[pltpu.VMEM((h, ML_DQK, ML_DV), F32), pltpu.VMEM((h, 1, ML_DQK), F32),
                        pltpu.VMEM((h, 1, LANES), F32)],
        compiler_params=_params("parallel", "arbitrary"),
    )(proj, proj, proj, proj, gates, bias, c0, n0, m0, g_h.reshape(1, vw))
    return outs


def _flat(x):
    return x.reshape(-1, x.shape[-1])


def _sb_layer(hp, hs, g, w_qkv, w_o, cache_k, cache_v):
    d = hp.shape[-1]
    w_q, w_kv = w_qkv[:, :d], w_qkv[:, d:]
    res = []
    for h, caches in ((hp, None), (hs, (cache_k, cache_v))):
        b, t, _ = h.shape
        (q,) = norm_matmul(_flat(h), g[0], w_q, [BF16])
        kv32, kv16 = norm_matmul_heads(h, g[0], w_kv, SB_HEADS)
        q = q.reshape(b, t, d)
        if caches is None:
            o = sb_attention_prompt(q, kv16)
        else:
            o = sb_attention_sample(q, kv16, caches[0], caches[1])
        h_new = matmul_norm_residual(_flat(o), w_o, g[1], _flat(h)).reshape(b, t, d)
        res.append((h_new, kv32[0], kv32[1]))
    return res


def _rw_layer(h, g, shift0, wkv0, p, w_o, ln_w, ln_b, r_k, chunk):
    b, t, d = h.shape
    r, lw, k, v, a, bb, gate, last = rwkv_project(h, shift0, g[0], p)
    y, s_fin = wkv_scan(r, lw, k, v, a, bb, gate, _pair_state(wkv0), ln_w, ln_b, r_k, chunk)
    h_new = matmul_norm_residual(_flat(y), w_o, g[1], _flat(h)).reshape(b, t, d)
    return h_new, last.reshape(b, d), _unpair_state(s_fin)


def _mla_layer(h, g, pos0, kv_cache, w):
    b, t, d = h.shape
    cos2, sin2 = _rope_tables(pos0 + jnp.arange(t))
    ckv, krope, kcat = mla_kv(h, g[0], w["dkv_ext"], w["g_kv"], cos2, sin2)
    (qd,) = norm_matmul(_flat(h), g[0], w["dq"], [BF16])
    (q_all,) = norm_matmul(qd, w["g_q"], w["uq_ext"], [BF16])
    q_full = mla_q_prepare(q_all.reshape(b, t, -1), w["uk_t"], cos2, sin2)
    if kv_cache is None:
        o = mla_attention(q_full, kcat, w["uv"], 0, True)
    else:
        c_ckv, c_kr = kv_cache
        kr16 = c_kr.astype(BF16)
        k_all = jnp.concatenate([jnp.concatenate([c_ckv.astype(BF16), kr16, kr16], axis=-1), kcat], axis=1)
        o = mla_attention(q_full, k_all, w["uv"], pos0, False)
    h_new = matmul_norm_residual(_flat(o), w["o"], g[1], _flat(h)).reshape(b, t, d)
    return h_new, ckv, krope


def _ml_layer(h, g, c0, n0, m0, chunk, w):
    b, t, d = h.shape
    (proj,) = norm_matmul(_flat(h), g[0], w["in"], [BF16])
    (gates,) = norm_matmul(_flat(h), g[0], w["if_pad"], [F32])
    hm, c, n, m = mlstm_scan(proj.reshape(b, t, -1), gates.reshape(b, t, LANES), w["b_pad"],
                             c0, n0.reshape(b, ML_HEADS, 1, ML_DQK),
                             jnp.broadcast_to(m0[:, :, None, None], (b, ML_HEADS, 1, LANES)), w["g_h"], chunk)
    h_new = matmul_norm_residual(_flat(hm), w["out"], g[1], _flat(h)).reshape(b, t, d)
    return h_new, c, n.reshape(b, ML_HEADS, ML_DQK), m[:, :, 0, 0]


def kernel(x_prompt, x_sample, mem_prompt, cache_sb_k, cache_sb_v, state_rwkv_shift, state_rwkv_wkv, cache_mla_ckv, cache_mla_krope, state_mlstm_c, state_mlstm_n, state_mlstm_m, cache_mem_k, cache_mem_v, norm_g, sb_w_qkv, sb_w_o, rw_mu, rw_w_r, rw_w_k, rw_w_v, rw_w_o, rw_w0, rw_w1, rw_w2, rw_a0, rw_a1, rw_a2, rw_g1, rw_g2, rw_k_k, rw_k_a, rw_r_k, rw_ln_w, rw_ln_b, mla_w_dq, mla_g_q, mla_w_uq, mla_w_dkv, mla_g_kv, mla_w_uk, mla_w_uv, mla_w_o, ml_w_in, ml_w_if, ml_b_if, ml_g_h, ml_w_out, w_mem_q, w_mem_kv, w_mem_o, w_ffn_up, w_ffn_down):
    bp, seq_p, d = x_prompt.shape
    bs, t_s, _ = x_sample.shape
    past = cache_sb_k.shape[1]
    depth = norm_g.shape[0]
    bf = lambda w: w.astype(BF16)
    pad_cols = lambda w, n: jnp.pad(w, ((0, 0), (0, n - w.shape[1])))
    pad_rows = lambda w, n: jnp.pad(w, ((0, n - w.shape[0]), (0, 0)))

    rw_p = dict(mu=rw_mu, w_r=bf(rw_w_r), w_k=bf(rw_w_k), w_v=bf(rw_w_v),
                w1=bf(pad_cols(rw_w1, RW_LORA_PAD)), w2=bf(pad_rows(rw_w2, RW_LORA_PAD)),
                a1=bf(pad_cols(rw_a1, RW_LORA_PAD)), a2=bf(pad_rows(rw_a2, RW_LORA_PAD)),
                g1=bf(rw_g1), g2=bf(rw_g2), w0=rw_w0, a0=rw_a0, k_k=rw_k_k, k_a=rw_k_a)

    kr_w = mla_w_dkv[:, MLA_KV_RANK:]
    uq = mla_w_uq.reshape(-1, MLA_HEADS, MLA_NOPE + MLA_ROPE)
    uq_rope = uq[:, :, MLA_NOPE:]
    rank_q = uq.shape[0]
    mla_w = dict(
        dkv_ext=bf(jnp.concatenate([mla_w_dkv[:, :MLA_KV_RANK], kr_w, kr_w, _rot_cols(kr_w), _rot_cols(kr_w)], axis=1)),
        g_kv=mla_g_kv, dq=bf(mla_w_dq), g_q=mla_g_q,
        uq_ext=bf(jnp.concatenate([uq[:, :, :MLA_NOPE].reshape(rank_q, -1), uq_rope.reshape(rank_q, -1),
                                   _rot_cols(uq_rope).reshape(rank_q, -1)], axis=1)),
        uk_t=bf(jnp.transpose(mla_w_uk, (1, 2, 0))),
        uv=bf(jnp.transpose(mla_w_uv, (1, 0, 2))),
        o=bf(mla_w_o))

    ml_w = {"in": bf(ml_w_in), "if_pad": bf(pad_cols(ml_w_if, LANES)),
            "b_pad": jnp.pad(ml_b_if, (0, LANES - ml_b_if.shape[0])).reshape(1, LANES),
            "g_h": ml_g_h, "out": bf(ml_w_out)}

    ffn_up, ffn_down = bf(w_ffn_up), bf(w_ffn_down)
    hp, hs = x_prompt, x_sample
    mem_k_list, mem_v_list = [], []
    outs = {}
    for layer in range(depth):
        g = norm_g[layer]
        kind = layer % 4
        if kind == 0:
            (hp, kp, vp), (hs, ks, vs) = _sb_layer(hp, hs, g, bf(sb_w_qkv), bf(sb_w_o), cache_sb_k, cache_sb_v)
            outs["sb"] = (kp, vp, ks, vs)
        elif kind == 1:
            n_h = d // RW_HD
            hp, sh_p, wkv_p = _rw_layer(hp, g, jnp.zeros((bp, d), F32), jnp.zeros((bp, n_h, RW_HD, RW_HD), F32),
                                        rw_p, bf(rw_w_o), rw_ln_w, rw_ln_b, rw_r_k, CHUNK)
            hs, sh_s, wkv_s = _rw_layer(hs, g, state_rwkv_shift, state_rwkv_wkv,
                                        rw_p, bf(rw_w_o), rw_ln_w, rw_ln_b, rw_r_k, CHUNK)
            outs["rw"] = (sh_p, wkv_p, sh_s, wkv_s)
        elif kind == 2:
            hp, ckv_p, kr_p = _mla_layer(hp, g, 0, None, mla_w)
            hs, ckv_s, kr_s = _mla_layer(hs, g, past, (cache_mla_ckv, cache_mla_krope), mla_w)
            outs["mla"] = (ckv_p, kr_p, ckv_s, kr_s)
        else:
            zc = jnp.zeros((bp, ML_HEADS, ML_DQK, ML_DV), F32)
            hp, c_p, n_p, m_p = _ml_layer(hp, g, zc, jnp.zeros((bp, ML_HEADS, ML_DQK), F32),
                                          jnp.zeros((bp, ML_HEADS), F32), CHUNK, ml_w)
            hs, c_s, n_s, m_s = _ml_layer(hs, g, state_mlstm_c, state_mlstm_n, state_mlstm_m, t_s, ml_w)
            outs["ml"] = (c_p, n_p, m_p, c_s, n_s, m_s)

        mkv, _ = norm_matmul_heads(mem_prompt, None, bf(w_mem_kv[layer]), MEM_HEADS)
        mk, mv = mkv[0], mkv[1]
        mem_k_list.append(mk)
        mem_v_list.append(mv)
        wq, wo = bf(w_mem_q[layer]), bf(w_mem_o[layer])
        hp = mem_attn_block(hp, g[2], wq, mkv, 0, mkv, 1, wo, g[3])
        hs = mem_attn_block(hs, g[2], wq, cache_mem_k, layer, cache_mem_v, layer, wo, g[3])

        hp = ffn_block(_flat(hp), g[4], ffn_up, ffn_down, layer, g[5]).reshape(bp, seq_p, d)
        hs = ffn_block(_flat(hs), g[4], ffn_up, ffn_down, layer, g[5]).reshape(bs, t_s, d)

    return (hp, hs) + outs["sb"] + outs["rw"] + outs["mla"] + outs["ml"] + (
        jnp.stack(mem_k_list, axis=0), jnp.stack(mem_v_list, axis=0))
```
